```python
import jax, jax.numpy as jnp
from jax import lax
import numpy as np

D_MODEL = 2048
BATCH = 1
SEQ = 16384
DEPTH = 1
DEC_BATCH = 128
DEC_SEQ = 4
PAST_LEN = 16384
PAGE_SIZE = 128

HEAD_DIM = 64
N_Q_HEADS = 16
N_KV_HEADS = 4
GQA_GROUP = N_Q_HEADS // N_KV_HEADS
ATTN_WIDTH = N_Q_HEADS * HEAD_DIM
KV_WIDTH = N_KV_HEADS * HEAD_DIM
WINDOW = 128
ATTN_BLOCK = 128
ROPE_THETA = 10000.0
GM_WIDTH = D_MODEL // 2
GM_GROUPS = 4
GM_GROUP_DIM = GM_WIDTH // GM_GROUPS
GM_CHUNK = 128
N_KEYS = 128
N_EXPERTS = N_KEYS * N_KEYS
PEER_HEADS = 8
PEER_TOPK = 16
PEER_KEY_DIM = 256
PEER_HALF = PEER_KEY_DIM // 2
PEER_BLOCK = 128
IN_WIDTH = ATTN_WIDTH + 2 * KV_WIDTH + 2 * GM_WIDTH + 2 * D_MODEL
SPLITS = [ATTN_WIDTH, ATTN_WIDTH + KV_WIDTH, ATTN_WIDTH + 2 * KV_WIDTH,
          ATTN_WIDTH + 2 * KV_WIDTH + GM_WIDTH, ATTN_WIDTH + 2 * KV_WIDTH + 2 * GM_WIDTH,
          ATTN_WIDTH + 2 * KV_WIDTH + 2 * GM_WIDTH + D_MODEL]
EPS = 1e-6
NEG_INF = -1e30

kernel_name = "hybrid_swa_sink_gmlp_peer_step"


def _rms(x, g):
    xf = x.astype(jnp.float32)
    y = xf * lax.rsqrt(jnp.mean(xf * xf, axis=-1, keepdims=True) + EPS)
    return (y * g.astype(jnp.float32)).astype(x.dtype)


def _layernorm(x, g, b):
    xf = x.astype(jnp.float32)
    mu = jnp.mean(xf, axis=-1, keepdims=True)
    var = jnp.mean(jnp.square(xf - mu), axis=-1, keepdims=True)
    y = (xf - mu) * lax.rsqrt(var + EPS)
    return (y * g.astype(jnp.float32) + b.astype(jnp.float32)).astype(x.dtype)


def _rope(x, pos):
    half = HEAD_DIM // 2
    inv = 1.0 / (ROPE_THETA ** (jnp.arange(half, dtype=jnp.float32) * (2.0 / HEAD_DIM)))
    ang = pos.astype(jnp.float32)[:, None] * inv[None, :]
    cos = jnp.cos(ang)[:, None, :]
    sin = jnp.sin(ang)[:, None, :]
    xf = x.astype(jnp.float32)
    x1, x2 = xf[..., :half], xf[..., half:]
    return jnp.concatenate([x1 * cos - x2 * sin, x2 * cos + x1 * sin], axis=-1).astype(x.dtype)


def _attend(q, k, v, mask, sinks):
    s = jnp.einsum('...qhgd,...khd->...hgqk', q, k).astype(jnp.float32) * (HEAD_DIM ** -0.5)
    s = jnp.where(mask, s, NEG_INF)
    sink = jnp.broadcast_to(sinks.astype(jnp.float32)[:, :, None, None], s.shape[:-1] + (1,))
    p = jax.nn.softmax(jnp.concatenate([s, sink], axis=-1), axis=-1)[..., :-1]
    return jnp.einsum('...hgqk,...khd->...qhgd', p.astype(v.dtype), v)


def _attn_prompt(q, k, v, sinks):
    b, t = q.shape[0], q.shape[1]
    nb = t // ATTN_BLOCK
    qb = q.reshape(b, nb, ATTN_BLOCK, N_KV_HEADS, GQA_GROUP, HEAD_DIM)
    kb = k.reshape(b, nb, ATTN_BLOCK, N_KV_HEADS, HEAD_DIM)
    vb = v.reshape(b, nb, ATTN_BLOCK, N_KV_HEADS, HEAD_DIM)
    padw = ((0, 0), (1, 0), (0, 0), (0, 0), (0, 0))
    kband = jnp.concatenate([jnp.pad(kb[:, :-1], padw), kb], axis=2)
    vband = jnp.concatenate([jnp.pad(vb[:, :-1], padw), vb], axis=2)
    blk = jnp.arange(nb)[:, None] * ATTN_BLOCK
    qpos = blk + jnp.arange(ATTN_BLOCK)[None, :]
    kpos = blk - ATTN_BLOCK + jnp.arange(2 * ATTN_BLOCK)[None, :]
    kp, qp = kpos[:, None, :], qpos[:, :, None]
    mask = (kp <= qp) & (kp > qp - WINDOW) & (kp >= 0)
    o = _attend(qb, kband, vband, mask[None, :, None, None], sinks)
    return o.reshape(b, t, ATTN_WIDTH), k[:, t - WINDOW:], v[:, t - WINDOW:]


def _attn_sample(q, k, v, cache_k, cache_v, sinks):
    db, s = q.shape[0], q.shape[1]
    cw = cache_k.shape[1]
    k_all = jnp.concatenate([cache_k.astype(k.dtype), k], axis=1)
    v_all = jnp.concatenate([cache_v.astype(v.dtype), v], axis=1)
    kpos = jnp.concatenate([PAST_LEN - cw + jnp.arange(cw), PAST_LEN + jnp.arange(s)])
    qpos = PAST_LEN + jnp.arange(s)
    kp, qp = kpos[None, :], qpos[:, None]
    mask = (kp <= qp) & (kp > qp - WINDOW)
    qg = q.reshape(db, s, N_KV_HEADS, GQA_GROUP, HEAD_DIM)
    o = _attend(qg, k_all, v_all, mask[None, None, None], sinks)
    n = k_all.shape[1]
    return o.reshape(db, s, ATTN_WIDTH), k_all[:, n - WINDOW:], v_all[:, n - WINDOW:]


def _chunk_mix(vb, wsm, bias):
    return jnp.einsum('gts,...sgd->...tgd', wsm, vb) + bias


def _peer(xt_all, w_q, keys1, keys2, u_tab, v_tab):
    n = xt_all.shape[0]
    nblk = -(-n // PEER_BLOCK)
    xb = jnp.pad(xt_all, ((0, nblk * PEER_BLOCK - n), (0, 0))).reshape(nblk, PEER_BLOCK, D_MODEL)

    def block(xt):
        q = (xt @ w_q).reshape(PEER_BLOCK, PEER_HEADS, 2, PEER_HALF)
        s1 = jnp.einsum('thd,kd->thk', q[:, :, 0], keys1).astype(jnp.float32)
        s2 = jnp.einsum('thd,kd->thk', q[:, :, 1], keys2).astype(jnp.float32)
        sv1, i1 = lax.top_k(s1, PEER_TOPK)
        sv2, i2 = lax.top_k(s2, PEER_TOPK)
        cand = (sv1[..., :, None] + sv2[..., None, :]).reshape(PEER_BLOCK, PEER_HEADS, PEER_TOPK * PEER_TOPK)
        sc, ci = lax.top_k(cand, PEER_TOPK)
        e1 = jnp.take_along_axis(i1, ci // PEER_TOPK, axis=-1)
        e2 = jnp.take_along_axis(i2, ci % PEER_TOPK, axis=-1)
        idx = e1 * N_KEYS + e2
        g = jax.nn.softmax(sc, axis=-1)
        a = jax.nn.gelu(jnp.einsum('td,thkd->thk', xt, u_tab[idx]), approximate=False)
        w = (g * a.astype(jnp.float32)).astype(xt.dtype)
        return jnp.einsum('thk,thkd->td', w, v_tab[idx])

    out = lax.map(block, xb).reshape(nblk * PEER_BLOCK, D_MODEL)
    return out[:n]


def _layer(x, prompt, ck, cv, g_mix, w_in, sinks, ln_g, ln_b, w_s, b_s, w_oa, w_og, w_out,
           g_ffn, pq, k1, k2, pu, pv):
    nbat, t = x.shape[0], x.shape[1]
    xn = _rms(x, g_mix)
    z = xn @ w_in
    q, k, v, u_g, v_g, ga, gb = jnp.split(z, SPLITS, axis=-1)
    q = q.reshape(nbat, t, N_Q_HEADS, HEAD_DIM)
    k = k.reshape(nbat, t, N_KV_HEADS, HEAD_DIM)
    v = v.reshape(nbat, t, N_KV_HEADS, HEAD_DIM)
    pos = jnp.arange(t) if prompt else PAST_LEN + jnp.arange(t)
    q = _rope(q, pos)
    k = _rope(k, pos)
    sinks_g = sinks.reshape(N_KV_HEADS, GQA_GROUP)
    if prompt:
        attn, new_k, new_v = _attn_prompt(q, k, v, sinks_g)
    else:
        attn, new_k, new_v = _attn_sample(q, k, v, ck, cv, sinks_g)
    u_g = jax.nn.gelu(u_g, approximate=False)
    v_g = _layernorm(jax.nn.gelu(v_g, approximate=False), ln_g, ln_b)
    wsm = w_s * jnp.tril(jnp.ones((GM_CHUNK, GM_CHUNK), w_s.dtype))
    bias = b_s.T[:, :, None]
    if prompt:
        vb = v_g.reshape(nbat, t // GM_CHUNK, GM_CHUNK, GM_GROUPS, GM_GROUP_DIM)
        mix = _chunk_mix(vb, wsm, bias)
        gm_state = v_g[:, t - GM_CHUNK:]
    else:
        vb = v_g.reshape(nbat, t, GM_GROUPS, GM_GROUP_DIM)
        mix = _chunk_mix(vb, wsm[:, :t, :t], bias[:t])
        gm_state = v_g
    gm = u_g * mix.reshape(nbat, t, GM_WIDTH)
    merged = jax.nn.sigmoid(ga) * (attn @ w_oa) + jax.nn.sigmoid(gb) * (gm @ w_og)
    h = x + merged @ w_out
    hn = _rms(h, g_ffn)
    h = h + _peer(hn.reshape(-1, D_MODEL), pq, k1, k2, pu, pv).reshape(h.shape)
    return h, new_k, new_v, gm_state


def setup_inputs(seed: int = 0) -> dict:
    key = jax.random.key(seed)
    ks = jax.random.split(key, 24)
    f32 = jnp.float32
    nrm = lambda k, shape, scale: jax.random.normal(k, shape, f32) * scale
    cache_rows = min(WINDOW, PAST_LEN)
    return {
        "x_prompt": nrm(ks[0], (BATCH, SEQ, D_MODEL), 1.0),
        "x_sample": nrm(ks[1], (DEC_BATCH, DEC_SEQ, D_MODEL), 1.0),
        "cache_k": nrm(ks[2], (DEPTH, DEC_BATCH, cache_rows, N_KV_HEADS, HEAD_DIM), 1.0),
        "cache_v": nrm(ks[3], (DEPTH, DEC_BATCH, cache_rows, N_KV_HEADS, HEAD_DIM), 1.0),
        "g_mix": 1.0 + nrm(ks[4], (DEPTH, D_MODEL), 0.05),
        "w_in": nrm(ks[5], (DEPTH, D_MODEL, IN_WIDTH), D_MODEL ** -0.5),
        "attn_sinks": nrm(ks[6], (DEPTH, N_Q_HEADS), 0.5),
        "gm_ln_g": 1.0 + nrm(ks[7], (DEPTH, GM_WIDTH), 0.05),
        "gm_ln_b": nrm(ks[8], (DEPTH, GM_WIDTH), 0.02),
        "gm_w_s": nrm(ks[9], (DEPTH, GM_GROUPS, GM_CHUNK, GM_CHUNK), GM_CHUNK ** -0.5),
        "gm_b_s": 1.0 + nrm(ks[10], (DEPTH, GM_GROUPS, GM_CHUNK), 0.1),
        "w_o_attn": nrm(ks[11], (DEPTH, ATTN_WIDTH, D_MODEL), ATTN_WIDTH ** -0.5),
        "w_o_gm": nrm(ks[12], (DEPTH, GM_WIDTH, D_MODEL), GM_WIDTH ** -0.5),
        "w_out": nrm(ks[13], (DEPTH, D_MODEL, D_MODEL), D_MODEL ** -0.5),
        "g_ffn": 1.0 + nrm(ks[14], (DEPTH, D_MODEL), 0.05),
        "peer_w_q": nrm(ks[15], (DEPTH, D_MODEL, PEER_HEADS * PEER_KEY_DIM), D_MODEL ** -0.5),
        "peer_keys1": nrm(ks[16], (DEPTH, N_KEYS, PEER_HALF), PEER_HALF ** -0.5),
        "peer_keys2": nrm(ks[17], (DEPTH, N_KEYS, PEER_HALF), PEER_HALF ** -0.5),
        "peer_u": nrm(ks[18], (DEPTH, N_EXPERTS, D_MODEL), D_MODEL ** -0.5),
        "peer_v": nrm(ks[19], (DEPTH, N_EXPERTS, D_MODEL), PEER_HEADS ** -0.5),
        "g_final": 1.0 + nrm(ks[20], (D_MODEL,), 0.05),
    }


def reference(x_prompt, x_sample, cache_k, cache_v, g_mix, w_in, attn_sinks, gm_ln_g, gm_ln_b,
              gm_w_s, gm_b_s, w_o_attn, w_o_gm, w_out, g_ffn, peer_w_q, peer_keys1, peer_keys2,
              peer_u, peer_v, g_final):
    hp, hs = x_prompt, x_sample
    kp_l, vp_l, ks_l, vs_l, gp_l, gs_l = [], [], [], [], [], []
    for l in range(DEPTH):
        lw = (g_mix[l], w_in[l], attn_sinks[l], gm_ln_g[l], gm_ln_b[l], gm_w_s[l], gm_b_s[l],
              w_o_attn[l], w_o_gm[l], w_out[l], g_ffn[l], peer_w_q[l], peer_keys1[l],
              peer_keys2[l], peer_u[l], peer_v[l])
        hp, kp, vp, gp = _layer(hp, True, None, None, *lw)
        hs, kk, vv, gs = _layer(hs, False, cache_k[l], cache_v[l], *lw)
        kp_l.append(kp); vp_l.append(vp); gp_l.append(gp)
        ks_l.append(kk); vs_l.append(vv); gs_l.append(gs)
    y_prompt = _rms(hp, g_final)
    y_sample = _rms(hs, g_final)
    return (y_prompt, y_sample, jnp.stack(kp_l), jnp.stack(vp_l), jnp.stack(ks_l), jnp.stack(vs_l),
            jnp.stack(gp_l), jnp.stack(gs_l))
```

```python
import functools

import jax
import jax.numpy as jnp
from jax import lax
from jax.experimental import pallas as pl
from jax.experimental.pallas import tpu as pltpu

F32 = jnp.float32
BF16 = jnp.bfloat16

D_MODEL = 2048
HEAD_DIM = 64
N_Q_HEADS = 16
N_KV_HEADS = 4
GQA_GROUP = N_Q_HEADS // N_KV_HEADS
ATTN_WIDTH = N_Q_HEADS * HEAD_DIM
KV_WIDTH = N_KV_HEADS * HEAD_DIM
WINDOW = 128
GM_WIDTH = D_MODEL // 2
GM_GROUPS = 4
GM_GROUP_DIM = GM_WIDTH // GM_GROUPS
GM_CHUNK = 128
N_KEYS = 128
PEER_HEADS = 8
PEER_TOPK = 16
PEER_HALF = 128
PICKS = PEER_HEADS * PEER_TOPK
ROPE_THETA = 10000.0
PAST_LEN = 16384
EPS = 1e-6
NEG_INF = -1e30

LANES = 128
D_CHUNKS = D_MODEL // LANES
TOK_BLOCK = 128
VMEM_LIMIT = 56 * 1024 * 1024

COL_GA, COL_GB = 0, D_MODEL
COL_Q = 2 * D_MODEL
COL_U = COL_Q + ATTN_WIDTH
COL_VG = COL_U + GM_WIDTH
COL_K = COL_VG + GM_WIDTH
COL_V = COL_K + KV_WIDTH
IN_WIDTH = COL_V + KV_WIDTH


def _params(sem, **kw):
    return pltpu.CompilerParams(dimension_semantics=sem, vmem_limit_bytes=VMEM_LIMIT, **kw)


def _gelu(x):
    return 0.5 * x * (1.0 + lax.erf(x * 0.7071067811865476))


def _rms_rows(x, g):
    return x * lax.rsqrt(jnp.mean(x * x, axis=-1, keepdims=True) + EPS) * g


def _rope(x, cos, sin):
    width = x.shape[1]
    lane = lax.broadcasted_iota(jnp.int32, x.shape, 1)
    first_half = (lane & (HEAD_DIM // 2)) == 0
    partner = jnp.where(first_half,
                        pltpu.roll(x, width - HEAD_DIM // 2, axis=1),
                        pltpu.roll(x, HEAD_DIM // 2, axis=1))
    reps = width // LANES
    if reps > 1:
        cos = jnp.concatenate([cos] * reps, axis=1)
        sin = jnp.concatenate([sin] * reps, axis=1)
    return x * cos + partner * sin


def _inproj_body(x_ref, g_ref, w_ref, z_ref, xn_ref):
    @pl.when(pl.program_id(1) == 0)
    def _():
        xn_ref[...] = _rms_rows(x_ref[...], g_ref[...]).astype(BF16)

    z_ref[...] = jnp.dot(xn_ref[...], w_ref[...], preferred_element_type=F32)


def _inproj(x, g, w, tm, tn):
    n = x.shape[0]
    return pl.pallas_call(
        _inproj_body,
        out_shape=jax.ShapeDtypeStruct((n, IN_WIDTH), F32),
        grid=(n // tm, IN_WIDTH // tn),
        in_specs=[pl.BlockSpec((tm, D_MODEL), lambda i, j: (i, 0)),
                  pl.BlockSpec((1, D_MODEL), lambda i, j: (0, 0)),
                  pl.BlockSpec((D_MODEL, tn), lambda i, j: (0, j))],
        out_specs=pl.BlockSpec((tm, tn), lambda i, j: (i, j)),
        scratch_shapes=[pltpu.VMEM((tm, D_MODEL), BF16)],
        compiler_params=_params(("arbitrary", "arbitrary")),
        name="inproj",
    )(x, g, w)


def _softmax_sink_pv(s, sink, v):
    m = jnp.maximum(jnp.max(s, axis=1, keepdims=True), sink)
    p = jnp.exp(s - m)
    den = jnp.sum(p, axis=1, keepdims=True) + jnp.exp(sink - m)
    return jnp.dot(p.astype(BF16), v, preferred_element_type=F32) / den


def _attn_prompt_body(sink_ref, q_ref, kc_ref, kp_ref, vc_ref, vp_ref, cc_ref, sc_ref,
                      cp_ref, sp_ref, o_ref, ko_ref, vo_ref):
    i = pl.program_id(0)
    q = (_rope(q_ref[...], cc_ref[...], sc_ref[...]) * (HEAD_DIM ** -0.5)).astype(BF16)
    kc = _rope(kc_ref[...], cc_ref[...], sc_ref[...])
    kp = _rope(kp_ref[...], cp_ref[...], sp_ref[...])
    vc = vc_ref[...]
    ko_ref[...] = kc
    vo_ref[...] = vc
    kband = jnp.concatenate([kp, kc], axis=0).astype(BF16)
    vband = jnp.concatenate([vp_ref[...], vc], axis=0).astype(BF16)
    row = lax.broadcasted_iota(jnp.int32, (TOK_BLOCK, 2 * TOK_BLOCK), 0)
    col = lax.broadcasted_iota(jnp.int32, (TOK_BLOCK, 2 * TOK_BLOCK), 1)
    mask = (col <= row + WINDOW) & (col > row) & ((i > 0) | (col >= TOK_BLOCK))
    outs = []
    for hq in range(N_Q_HEADS):
        h = hq // GQA_GROUP
        qh = q[:, hq * HEAD_DIM:(hq + 1) * HEAD_DIM]
        kh = kband[:, h * HEAD_DIM:(h + 1) * HEAD_DIM]
        vh = vband[:, h * HEAD_DIM:(h + 1) * HEAD_DIM]
        s = lax.dot_general(qh, kh, (((1,), (1,)), ((), ())), preferred_element_type=F32)
        s = jnp.where(mask, s, NEG_INF)
        outs.append(_softmax_sink_pv(s, sink_ref[hq], vh))
    o_ref[...] = jnp.concatenate(outs, axis=1).astype(BF16)


def _attn_prompt(z, cos, sin, sinks, n_prompt):
    nb = n_prompt // TOK_BLOCK
    prev = lambda i: jnp.maximum(i - 1, 0)
    qb, kb, vb = COL_Q // ATTN_WIDTH, COL_K // KV_WIDTH, COL_V // KV_WIDTH
    return pl.pallas_call(
        _attn_prompt_body,
        out_shape=(jax.ShapeDtypeStruct((n_prompt, ATTN_WIDTH), BF16),
                   jax.ShapeDtypeStruct((TOK_BLOCK, KV_WIDTH), F32),
                   jax.ShapeDtypeStruct((TOK_BLOCK, KV_WIDTH), F32)),
        grid=(nb,),
        in_specs=[pl.BlockSpec(memory_space=pltpu.SMEM),
                  pl.BlockSpec((TOK_BLOCK, ATTN_WIDTH), lambda i: (i, qb)),
                  pl.BlockSpec((TOK_BLOCK, KV_WIDTH), lambda i: (i, kb)),
                  pl.BlockSpec((TOK_BLOCK, KV_WIDTH), lambda i: (prev(i), kb)),
                  pl.BlockSpec((TOK_BLOCK, KV_WIDTH), lambda i: (i, vb)),
                  pl.BlockSpec((TOK_BLOCK, KV_WIDTH), lambda i: (prev(i), vb)),
                  pl.BlockSpec((TOK_BLOCK, LANES), lambda i: (i, 0)),
                  pl.BlockSpec((TOK_BLOCK, LANES), lambda i: (i, 0)),
                  pl.BlockSpec((TOK_BLOCK, LANES), lambda i: (prev(i), 0)),
                  pl.BlockSpec((TOK_BLOCK, LANES), lambda i: (prev(i), 0))],
        out_specs=(pl.BlockSpec((TOK_BLOCK, ATTN_WIDTH), lambda i: (i, 0)),
                   pl.BlockSpec((TOK_BLOCK, KV_WIDTH), lambda i: (0, 0)),
                   pl.BlockSpec((TOK_BLOCK, KV_WIDTH), lambda i: (0, 0))),
        compiler_params=_params(("arbitrary",)),
        name="attn_prompt",
    )(sinks, z, z, z, z, z, cos, sin, cos, sin)


def _attn_sample_body(sink_ref, q_ref, k_ref, v_ref, cos_ref, sin_ref, ck_ref, cv_ref,
                      o_ref, ko_ref, vo_ref, *, seqs, dec):
    q = _rope(q_ref[...], cos_ref[...], sin_ref[...]) * (HEAD_DIM ** -0.5)
    kn = _rope(k_ref[...], cos_ref[...], sin_ref[...])
    vn = v_ref[...]
    rows = GQA_GROUP * dec
    tok_c = lax.broadcasted_iota(jnp.int32, (rows, WINDOW), 0) % dec
    col_c = lax.broadcasted_iota(jnp.int32, (rows, WINDOW), 1)
    mask_c = col_c > tok_c
    tok_n = lax.broadcasted_iota(jnp.int32, (rows, dec), 0) % dec
    col_n = lax.broadcasted_iota(jnp.int32, (rows, dec), 1)
    mask_n = col_n <= tok_n
    grp = lax.broadcasted_iota(jnp.int32, (rows, 1), 0) // dec
    out_rows = []
    for b in range(seqs):
        r0 = b * dec
        ck = ck_ref[b]
        cv = cv_ref[b]
        ko_ref[b, 0:WINDOW - dec, :] = ck[dec:, :]
        ko_ref[b, WINDOW - dec:WINDOW, :] = kn[r0:r0 + dec, :]
        vo_ref[b, 0:WINDOW - dec, :] = cv[dec:, :]
        vo_ref[b, WINDOW - dec:WINDOW, :] = vn[r0:r0 + dec, :]
        heads = []
        for h in range(N_KV_HEADS):
            lo = h * HEAD_DIM
            qs = jnp.concatenate(
                [q[r0:r0 + dec, (h * GQA_GROUP + g) * HEAD_DIM:(h * GQA_GROUP + g + 1) * HEAD_DIM]
                 for g in range(GQA_GROUP)], axis=0).astype(BF16)
            sink = jnp.zeros((rows, 1), F32)
            for g in range(GQA_GROUP):
                sink = jnp.where(grp == g, sink_ref[h * GQA_GROUP + g], sink)
            kch = ck[:, lo:lo + HEAD_DIM].astype(BF16)
            vch = cv[:, lo:lo + HEAD_DIM].astype(BF16)
            knh = kn[r0:r0 + dec, lo:lo + HEAD_DIM].astype(BF16)
            vnh = vn[r0:r0 + dec, lo:lo + HEAD_DIM].astype(BF16)
            dn = (((1,), (1,)), ((), ()))
            s_c = jnp.where(mask_c, lax.dot_general(qs, kch, dn, preferred_element_type=F32), NEG_INF)
            s_n = jnp.where(mask_n, lax.dot_general(qs, knh, dn, preferred_element_type=F32), NEG_INF)
            m = jnp.maximum(jnp.maximum(jnp.max(s_c, axis=1, keepdims=True),
                                        jnp.max(s_n, axis=1, keepdims=True)), sink)
            p_c = jnp.exp(s_c - m)
            p_n = jnp.exp(s_n - m)
            den = (jnp.sum(p_c, axis=1, keepdims=True) + jnp.sum(p_n, axis=1, keepdims=True)
                   + jnp.exp(sink - m))
            o = (jnp.dot(p_c.astype(BF16), vch, preferred_element_type=F32)
                 + jnp.dot(p_n.astype(BF16), vnh, preferred_element_type=F32)) / den
            heads.extend(o[g * dec:(g + 1) * dec, :] for g in range(GQA_GROUP))
        out_rows.append(jnp.concatenate(heads, axis=1))
    o_ref[...] = jnp.concatenate(out_rows, axis=0).astype(BF16)


def _attn_sample(z, cos, sin, sinks, cache_k, cache_v, n_prompt, dec_batch, dec):
    seqs = 8
    rb = seqs * dec
    base = n_prompt // rb
    qb, kb, vb = COL_Q // ATTN_WIDTH, COL_K // KV_WIDTH, COL_V // KV_WIDTH
    return pl.pallas_call(
        functools.partial(_attn_sample_body, seqs=seqs, dec=dec),
        out_shape=(jax.ShapeDtypeStruct((dec_batch * dec, ATTN_WIDTH), BF16),
                   jax.ShapeDtypeStruct((dec_batch, WINDOW, KV_WIDTH), F32),
                   jax.ShapeDtypeStruct((dec_batch, WINDOW, KV_WIDTH), F32)),
        grid=(dec_batch // seqs,),
        in_specs=[pl.BlockSpec(memory_space=pltpu.SMEM),
                  pl.BlockSpec((rb, ATTN_WIDTH), lambda i: (base + i, qb)),
                  pl.BlockSpec((rb, KV_WIDTH), lambda i: (base + i, kb)),
                  pl.BlockSpec((rb, KV_WIDTH), lambda i: (base + i, vb)),
                  pl.BlockSpec((rb, LANES), lambda i: (base + i, 0)),
                  pl.BlockSpec((rb, LANES), lambda i: (base + i, 0)),
                  pl.BlockSpec((seqs, WINDOW, KV_WIDTH), lambda i: (i, 0, 0)),
                  pl.BlockSpec((seqs, WINDOW, KV_WIDTH), lambda i: (i, 0, 0))],
        out_specs=(pl.BlockSpec((rb, ATTN_WIDTH), lambda i: (i, 0)),
                   pl.BlockSpec((seqs, WINDOW, KV_WIDTH), lambda i: (i, 0, 0)),
                   pl.BlockSpec((seqs, WINDOW, KV_WIDTH), lambda i: (i, 0, 0))),
        compiler_params=_params(("arbitrary",)),
        name="attn_sample",
    )(sinks, z, z, z, cos, sin, cache_k, cache_v)


def _gmlp_body(u_ref, v_ref, lg_ref, lb_ref, ws_ref, bs_ref, gm_ref, vs_ref, *, n_prompt_blocks, dec):
    i = pl.program_id(0)
    u = _gelu(u_ref[...])
    v = _gelu(v_ref[...])
    mu = jnp.mean(v, axis=-1, keepdims=True)
    var = jnp.mean(jnp.square(v - mu), axis=-1, keepdims=True)
    v = (v - mu) * lax.rsqrt(var + EPS) * lg_ref[...] + lb_ref[...]
    vs_ref[...] = v
    row = lax.broadcasted_iota(jnp.int32, (GM_CHUNK, GM_CHUNK), 0)
    col = lax.broadcasted_iota(jnp.int32, (GM_CHUNK, GM_CHUNK), 1)
    shift = jnp.where(i < n_prompt_blocks, GM_CHUNK.bit_length() - 1, dec.bit_length() - 1)
    mask = (col <= row) & ((row >> shift) == (col >> shift))
    vb = v.astype(BF16)
    outs = []
    for g in range(GM_GROUPS):
        w = jnp.where(mask, ws_ref[0, g], 0.0).astype(BF16)
        lo = g * GM_GROUP_DIM
        mix = jnp.dot(w, vb[:, lo:lo + GM_GROUP_DIM], preferred_element_type=F32) + bs_ref[0, g]
        outs.append(u[:, lo:lo + GM_GROUP_DIM] * mix)
    gm_ref[...] = jnp.concatenate(outs, axis=1).astype(BF16)


def _gmlp(z, ln_g, ln_b, w_mix, b_mix, n_tokens, n_prompt, dec):
    nb = n_tokens // TOK_BLOCK
    npb = n_prompt // TOK_BLOCK
    ub, vb = COL_U // GM_WIDTH, COL_VG // GM_WIDTH
    variant = lambda i: jnp.minimum(i // npb, 1)
    return pl.pallas_call(
        functools.partial(_gmlp_body, n_prompt_blocks=npb, dec=dec),
        out_shape=(jax.ShapeDtypeStruct((n_tokens, GM_WIDTH), BF16),
                   jax.ShapeDtypeStruct(((nb - npb + 1) * TOK_BLOCK, GM_WIDTH), F32)),
        grid=(nb,),
        in_specs=[pl.BlockSpec((TOK_BLOCK, GM_WIDTH), lambda i: (i, ub)),
                  pl.BlockSpec((TOK_BLOCK, GM_WIDTH), lambda i: (i, vb)),
                  pl.BlockSpec((1, GM_WIDTH), lambda i: (0, 0)),
                  pl.BlockSpec((1, GM_WIDTH), lambda i: (0, 0)),
                  pl.BlockSpec((1, GM_GROUPS, GM_CHUNK, GM_CHUNK), lambda i: (variant(i), 0, 0, 0)),
                  pl.BlockSpec((1, GM_GROUPS, GM_CHUNK, 1), lambda i: (variant(i), 0, 0, 0))],
        out_specs=(pl.BlockSpec((TOK_BLOCK, GM_WIDTH), lambda i: (i, 0)),
                   pl.BlockSpec((TOK_BLOCK, GM_WIDTH), lambda i: (jnp.maximum(i - (npb - 1), 0), 0))),
        compiler_params=_params(("arbitrary",)),
        name="gmlp",
    )(z, z, ln_g, ln_b, w_mix, b_mix)


def _merge_body(attn_ref, gm_ref, ga_ref, gb_ref, x_ref, woa_ref, wog_ref, wout_ref, gf_ref,
                h_ref, hn_ref):
    a = jnp.dot(attn_ref[...], woa_ref[...], preferred_element_type=F32)
    b = jnp.dot(gm_ref[...], wog_ref[...], preferred_element_type=F32)
    merged = jax.nn.sigmoid(ga_ref[...]) * a + jax.nn.sigmoid(gb_ref[...]) * b
    h = x_ref[...] + jnp.dot(merged.astype(BF16), wout_ref[...], preferred_element_type=F32)
    h_ref[...] = h
    hn_ref[...] = _rms_rows(h, gf_ref[...])


def _merge(attn, gm, z, x, w_oa, w_og, w_out, g_ffn, tm):
    n = x.shape[0]
    once = pl.Buffered(1)
    return pl.pallas_call(
        _merge_body,
        out_shape=(jax.ShapeDtypeStruct((n, D_MODEL), F32), jax.ShapeDtypeStruct((n, D_MODEL), F32)),
        grid=(n // tm,),
        in_specs=[pl.BlockSpec((tm, ATTN_WIDTH), lambda i: (i, 0)),
                  pl.BlockSpec((tm, GM_WIDTH), lambda i: (i, 0)),
                  pl.BlockSpec((tm, D_MODEL), lambda i: (i, COL_GA // D_MODEL)),
                  pl.BlockSpec((tm, D_MODEL), lambda i: (i, COL_GB // D_MODEL)),
                  pl.BlockSpec((tm, D_MODEL), lambda i: (i, 0)),
                  pl.BlockSpec((ATTN_WIDTH, D_MODEL), lambda i: (0, 0), pipeline_mode=once),
                  pl.BlockSpec((GM_WIDTH, D_MODEL), lambda i: (0, 0), pipeline_mode=once),
                  pl.BlockSpec((D_MODEL, D_MODEL), lambda i: (0, 0), pipeline_mode=once),
                  pl.BlockSpec((1, D_MODEL), lambda i: (0, 0))],
        out_specs=(pl.BlockSpec((tm, D_MODEL), lambda i: (i, 0)),
                   pl.BlockSpec((tm, D_MODEL), lambda i: (i, 0))),
        compiler_params=_params(("arbitrary",)),
        name="merge",
    )(attn, gm, z, z, x, w_oa, w_og, w_out, g_ffn)


def _topk_rows(s, k):
    n = s.shape[0]
    rows = lax.broadcasted_iota(jnp.int32, s.shape, 0)
    vals, ids = [], []
    for _ in range(k):
        m = jnp.max(s, axis=0, keepdims=True)
        am = jnp.min(jnp.where(s == m, rows, n), axis=0, keepdims=True)
        vals.append(m)
        ids.append(am)
        s = jnp.where(rows == am, -jnp.inf, s)
    return jnp.concatenate(vals, axis=0), jnp.concatenate(ids, axis=0)


def _take_rows(table, sel, n):
    out = jnp.zeros(sel.shape, table.dtype)
    for r in range(n):
        out = jnp.where(sel == r, table[r:r + 1, :], out)
    return out


def _route_body(hn_ref, wq_ref, k1_ref, k2_ref, idx_ref, gate_ref):
    q = jnp.dot(hn_ref[...].astype(BF16), wq_ref[...], preferred_element_type=F32).astype(BF16)
    k1 = k1_ref[...]
    k2 = k2_ref[...]
    dn = (((1,), (1,)), ((), ()))
    idx_out, gate_out = [], []
    for h in range(PEER_HEADS):
        lo = h * 2 * PEER_HALF
        s1 = lax.dot_general(k1, q[:, lo:lo + PEER_HALF], dn, preferred_element_type=F32)
        s2 = lax.dot_general(k2, q[:, lo + PEER_HALF:lo + 2 * PEER_HALF], dn, preferred_element_type=F32)
        v1, i1 = _topk_rows(s1, PEER_TOPK)
        v2, i2 = _topk_rows(s2, PEER_TOPK)
        cand = jnp.concatenate([v1[r:r + 1, :] + v2 for r in range(PEER_TOPK)], axis=0)
        sc, ci = _topk_rows(cand, PEER_TOPK)
        e1 = _take_rows(i1, ci // PEER_TOPK, PEER_TOPK)
        e2 = _take_rows(i2, ci % PEER_TOPK, PEER_TOPK)
        idx_out.append(e1 * N_KEYS + e2)
        p = jnp.exp(sc - sc[0:1, :])
        gate_out.append(p / jnp.sum(p, axis=0, keepdims=True))
    idx_ref[0] = jnp.concatenate(idx_out, axis=0)
    gate_ref[0] = jnp.concatenate(gate_out, axis=0)


def _route(hn, w_q, keys1, keys2):
    nb = hn.shape[0] // TOK_BLOCK
    return pl.pallas_call(
        _route_body,
        out_shape=(jax.ShapeDtypeStruct((nb, PICKS, TOK_BLOCK), jnp.int32),
                   jax.ShapeDtypeStruct((nb, PICKS, TOK_BLOCK), F32)),
        grid=(nb,),
        in_specs=[pl.BlockSpec((TOK_BLOCK, D_MODEL), lambda i: (i, 0)),
                  pl.BlockSpec((D_MODEL, D_MODEL), lambda i: (0, 0), pipeline_mode=pl.Buffered(1)),
                  pl.BlockSpec((N_KEYS, PEER_HALF), lambda i: (0, 0)),
                  pl.BlockSpec((N_KEYS, PEER_HALF), lambda i: (0, 0))],
        out_specs=(pl.BlockSpec((1, PICKS, TOK_BLOCK), lambda i: (i, 0, 0)),
                   pl.BlockSpec((1, PICKS, TOK_BLOCK), lambda i: (i, 0, 0))),
        compiler_params=_params(("arbitrary",)),
        name="route",
    )(hn, w_q, keys1, keys2)


def _expert_copy(tab_ref, gbuf, sems, slot, pick, expert):
    return pltpu.make_async_copy(tab_ref.at[expert], gbuf.at[slot, :, pl.ds(pick, 1), :], sems.at[slot])


def _peer_body(idx_ref, gate_ref, hn_ref, h_ref, gfin_ref, tab_ref, y_ref, gbuf, sems, acc_ref):
    def issue(tok, slot):
        for j in range(PICKS):
            _expert_copy(tab_ref, gbuf, sems, slot, j, idx_ref[0, j, tok]).start()

    def wait_all(slot):
        pltpu.make_async_copy(gbuf.at[slot], gbuf.at[slot], sems.at[slot]).wait()

    lane = lax.broadcasted_iota(jnp.int32, (PICKS, TOK_BLOCK), 1)

    def compute(tok, slot):
        x = hn_ref[pl.ds(tok, 1), :]
        acc = jnp.zeros((PICKS, LANES), F32)
        for c in range(D_CHUNKS):
            u = lax.bitcast_convert_type(gbuf[slot, c] & jnp.int32(-65536), F32)
            acc = acc + u * x[:, c * LANES:(c + 1) * LANES]
        a = jnp.sum(acc, axis=1, keepdims=True)
        gate = jnp.sum(jnp.where(lane == tok, gate_ref[0], 0.0), axis=1, keepdims=True)
        w = gate * _gelu(a)
        outs = []
        for c in range(D_CHUNKS):
            v = lax.bitcast_convert_type(gbuf[slot, c] << 16, F32)
            outs.append(jnp.sum(w * v, axis=0, keepdims=True))
        acc_ref[pl.ds(tok, 1), :] = jnp.concatenate(outs, axis=1)

    issue(0, 0)

    def pair(p, carry):
        t0 = 2 * p
        issue(t0 + 1, 1)
        wait_all(0)
        compute(t0, 0)

        @pl.when(t0 + 2 < TOK_BLOCK)
        def _():
            issue(t0 + 2, 0)

        wait_all(1)
        compute(t0 + 1, 1)
        return carry

    lax.fori_loop(0, TOK_BLOCK // 2, pair, 0)
    y_ref[...] = _rms_rows(h_ref[...] + acc_ref[...], gfin_ref[...])


def _peer(idx, gate, hn, h, g_final, table):
    n = hn.shape[0]
    nb = n // TOK_BLOCK
    return pl.pallas_call(
        _peer_body,
        out_shape=jax.ShapeDtypeStruct((n, D_MODEL), F32),
        grid=(nb,),
        in_specs=[pl.BlockSpec((1, PICKS, TOK_BLOCK), lambda i: (i, 0, 0), memory_space=pltpu.SMEM),
                  pl.BlockSpec((1, PICKS, TOK_BLOCK), lambda i: (i, 0, 0)),
                  pl.BlockSpec((TOK_BLOCK, D_MODEL), lambda i: (i, 0)),
                  pl.BlockSpec((TOK_BLOCK, D_MODEL), lambda i: (i, 0)),
                  pl.BlockSpec((1, D_MODEL), lambda i: (0, 0)),
                  pl.BlockSpec(memory_space=pl.ANY)],
        out_specs=pl.BlockSpec((TOK_BLOCK, D_MODEL), lambda i: (i, 0)),
        scratch_shapes=[pltpu.VMEM((2, D_CHUNKS, PICKS, LANES), jnp.int32),
                        pltpu.SemaphoreType.DMA((2,)),
                        pltpu.VMEM((TOK_BLOCK, D_MODEL), F32)],
        compiler_params=_params(("arbitrary",)),
        name="peer",
    )(idx, gate, hn, h, g_final, table)


def _pack_expert_table(peer_u, peer_v):
    ub = lax.bitcast_convert_type(peer_u.astype(BF16), jnp.uint16).astype(jnp.uint32)
    vb = lax.bitcast_convert_type(peer_v.astype(BF16), jnp.uint16).astype(jnp.uint32)
    words = lax.bitcast_convert_type((ub << 16) | vb, jnp.int32)
    return words.reshape(words.shape[0], D_CHUNKS, 1, LANES)


def _rope_tables(n_prompt, n_sample, dec):
    half = HEAD_DIM // 2
    inv = 1.0 / (ROPE_THETA ** (jnp.arange(half, dtype=F32) * (2.0 / HEAD_DIM)))
    pos = jnp.concatenate([jnp.arange(n_prompt), PAST_LEN + jnp.arange(n_sample) % dec])
    ang = pos.astype(F32)[:, None] * inv[None, :]
    c, s = jnp.cos(ang), jnp.sin(ang)
    return jnp.concatenate([c, c, c, c], axis=1), jnp.concatenate([-s, s, -s, s], axis=1)


def _row_tile(n, cap):
    t = cap
    while n % t:
        t //= 2
    return t


def kernel(x_prompt, x_sample, cache_k, cache_v, g_mix, w_in, attn_sinks, gm_ln_g, gm_ln_b,
           gm_w_s, gm_b_s, w_o_attn, w_o_gm, w_out, g_ffn, peer_w_q, peer_keys1, peer_keys2,
           peer_u, peer_v, g_final):
    depth = g_mix.shape[0]
    assert depth == 1 and x_prompt.shape[0] == 1
    n_prompt = x_prompt.shape[1]
    dec_batch, dec = x_sample.shape[0], x_sample.shape[1]
    n_sample = dec_batch * dec
    n = n_prompt + n_sample
    assert n_prompt % TOK_BLOCK == 0 and n_sample % TOK_BLOCK == 0 and TOK_BLOCK % dec == 0
    assert cache_k.shape[2] == WINDOW and dec_batch % 8 == 0

    x = jnp.concatenate([x_prompt[0], x_sample.reshape(n_sample, D_MODEL)], axis=0)
    wi = w_in[0]
    a, kv, gmw = ATTN_WIDTH, KV_WIDTH, GM_WIDTH
    w_perm = jnp.concatenate([wi[:, a + 2 * kv + 2 * gmw:], wi[:, :a], wi[:, a + 2 * kv:a + 2 * kv + 2 * gmw],
                              wi[:, a:a + 2 * kv]], axis=1).astype(BF16)
    cos, sin = _rope_tables(n_prompt, n_sample, dec)
    reps = TOK_BLOCK // dec
    w_mix = jnp.stack([gm_w_s[0], jnp.tile(gm_w_s[0][:, :dec, :dec], (1, reps, reps))])
    b_mix = jnp.stack([gm_b_s[0], jnp.tile(gm_b_s[0][:, :dec], (1, reps))])[..., None]
    table = _pack_expert_table(peer_u[0], peer_v[0])

    z = _inproj(x, g_mix, w_perm, _row_tile(n, 512), 512)
    attn_p, k_p, v_p = _attn_prompt(z, cos, sin, attn_sinks[0], n_prompt)
    ck = cache_k[0].reshape(dec_batch, WINDOW, KV_WIDTH)
    cv = cache_v[0].reshape(dec_batch, WINDOW, KV_WIDTH)
    attn_s, k_s, v_s = _attn_sample(z, cos, sin, attn_sinks[0], ck, cv, n_prompt, dec_batch, dec)
    attn = jnp.concatenate([attn_p, attn_s], axis=0)
    gm, v_state = _gmlp(z, gm_ln_g, gm_ln_b, w_mix, b_mix, n, n_prompt, dec)
    h, hn = _merge(attn, gm, z, x, w_o_attn[0].astype(BF16), w_o_gm[0].astype(BF16),
                   w_out[0].astype(BF16), g_ffn, _row_tile(n, 256))
    idx, gate = _route(hn, peer_w_q[0].astype(BF16), peer_keys1[0].astype(BF16),
                       peer_keys2[0].astype(BF16))
    y = _peer(idx, gate, hn, h, g_final.reshape(1, D_MODEL), table)

    kv_shape = (1, 1, WINDOW, N_KV_HEADS, HEAD_DIM)
    skv_shape = (1, dec_batch, WINDOW, N_KV_HEADS, HEAD_DIM)
    return (y[:n_prompt].reshape(1, n_prompt, D_MODEL),
            y[n_prompt:].reshape(dec_batch, dec, D_MODEL),
            k_p.reshape(kv_shape), v_p.reshape(kv_shape),
            k_s.reshape(skv_shape), v_s.reshape(skv_shape),
            v_state[:TOK_BLOCK].reshape(1, 1, GM_CHUNK, GM_WIDTH),
            v_state[TOK_BLOCK:].reshape(1, dec_batch, dec, GM_WIDTH))
```

```python
import functools

import jax
import jax.numpy as jnp
from jax import lax
from jax.experimental import pallas as pl
from jax.experimental.pallas import tpu as pltpu

F32 = jnp.float32
BF16 = jnp.bfloat16

D_MODEL = 2048
HEAD_DIM = 64
N_Q_HEADS = 16
N_KV_HEADS = 4
GQA_GROUP = N_Q_HEADS // N_KV_HEADS
ATTN_WIDTH = N_Q_HEADS * HEAD_DIM
KV_WIDTH = N_KV_HEADS * HEAD_DIM
WINDOW = 128
GM_WIDTH = D_MODEL // 2
GM_GROUPS = 4
GM_GROUP_DIM = GM_WIDTH // GM_GROUPS
GM_CHUNK = 128
N_KEYS = 128
PEER_HEADS = 8
PEER_TOPK = 16
PEER_HALF = 128
PICKS = PEER_HEADS * PEER_TOPK
ROPE_THETA = 10000.0
PAST_LEN = 16384
EPS = 1e-6
NEG_INF = -1e30

LANES = 128
D_CHUNKS = D_MODEL // LANES
TOK_BLOCK = 128
VMEM_LIMIT = 56 * 1024 * 1024

COL_GA, COL_GB = 0, D_MODEL
COL_Q = 2 * D_MODEL
COL_U = COL_Q + ATTN_WIDTH
COL_VG = COL_U + GM_WIDTH
COL_K = COL_VG + GM_WIDTH
COL_V = COL_K + KV_WIDTH
IN_WIDTH = COL_V + KV_WIDTH


def _params(sem, **kw):
    return pltpu.CompilerParams(dimension_semantics=sem, vmem_limit_bytes=VMEM_LIMIT, **kw)


def _gelu(x):
    return 0.5 * x * (1.0 + lax.erf(x * 0.7071067811865476))


def _rms_rows(x, g):
    return x * lax.rsqrt(jnp.mean(x * x, axis=-1, keepdims=True) + EPS) * g


def _rope(x, cos, sin):
    width = x.shape[1]
    lane = lax.broadcasted_iota(jnp.int32, x.shape, 1)
    first_half = (lane & (HEAD_DIM // 2)) == 0
    partner = jnp.where(first_half,
                        pltpu.roll(x, width - HEAD_DIM // 2, axis=1),
                        pltpu.roll(x, HEAD_DIM // 2, axis=1))
    reps = width // LANES
    if reps > 1:
        cos = jnp.concatenate([cos] * reps, axis=1)
        sin = jnp.concatenate([sin] * reps, axis=1)
    return x * cos + partner * sin


def _inproj_body(x_ref, g_ref, w_ref, z_ref, xn_ref):
    @pl.when(pl.program_id(1) == 0)
    def _():
        xn_ref[...] = _rms_rows(x_ref[...], g_ref[...]).astype(BF16)

    z_ref[...] = jnp.dot(xn_ref[...], w_ref[...], preferred_element_type=F32)


def _inproj(x, g, w, tm, tn):
    n = x.shape[0]
    return pl.pallas_call(
        _inproj_body,
        out_shape=jax.ShapeDtypeStruct((n, IN_WIDTH), F32),
        grid=(n // tm, IN_WIDTH // tn),
        in_specs=[pl.BlockSpec((tm, D_MODEL), lambda i, j: (i, 0)),
                  pl.BlockSpec((1, D_MODEL), lambda i, j: (0, 0)),
                  pl.BlockSpec((D_MODEL, tn), lambda i, j: (0, j))],
        out_specs=pl.BlockSpec((tm, tn), lambda i, j: (i, j)),
        scratch_shapes=[pltpu.VMEM((tm, D_MODEL), BF16)],
        compiler_params=_params(("arbitrary", "arbitrary")),
        name="inproj",
    )(x, g, w)


def _softmax_sink_pv(s, sink, v):
    m = jnp.maximum(jnp.max(s, axis=1, keepdims=True), sink)
    p = jnp.exp(s - m)
    den = jnp.sum(p, axis=1, keepdims=True) + jnp.exp(sink - m)
    return jnp.dot(p.astype(BF16), v, preferred_element_type=F32) / den


def _attn_prompt_body(sink_ref, q_ref, kc_ref, kp_ref, vc_ref, vp_ref, cc_ref, sc_ref,
                      cp_ref, sp_ref, o_ref, ko_ref, vo_ref):
    i = pl.program_id(0)
    q = (_rope(q_ref[...], cc_ref[...], sc_ref[...]) * (HEAD_DIM ** -0.5)).astype(BF16)
    kc = _rope(kc_ref[...], cc_ref[...], sc_ref[...])
    kp = _rope(kp_ref[...], cp_ref[...], sp_ref[...])
    vc = vc_ref[...]
    ko_ref[...] = kc
    vo_ref[...] = vc
    kband = jnp.concatenate([kp, kc], axis=0).astype(BF16)
    vband = jnp.concatenate([vp_ref[...], vc], axis=0).astype(BF16)
    row = lax.broadcasted_iota(jnp.int32, (TOK_BLOCK, 2 * TOK_BLOCK), 0)
    col = lax.broadcasted_iota(jnp.int32, (TOK_BLOCK, 2 * TOK_BLOCK), 1)
    mask = (col <= row + WINDOW) & (col > row) & ((i > 0) | (col >= TOK_BLOCK))
    outs = []
    for hq in range(N_Q_HEADS):
        h = hq // GQA_GROUP
        qh = q[:, hq * HEAD_DIM:(hq + 1) * HEAD_DIM]
        kh = kband[:, h * HEAD_DIM:(h + 1) * HEAD_DIM]
        vh = vband[:, h * HEAD_DIM:(h + 1) * HEAD_DIM]
        s = lax.dot_general(qh, kh, (((1,), (1,)), ((), ())), preferred_element_type=F32)
        s = jnp.where(mask, s, NEG_INF)
        outs.append(_softmax_sink_pv(s, sink_ref[hq], vh))
    o_ref[...] = jnp.concatenate(outs, axis=1).astype(BF16)


def _attn_prompt(z, cos, sin, sinks, n_prompt):
    nb = n_prompt // TOK_BLOCK
    prev = lambda i: jnp.maximum(i - 1, 0)
    qb, kb, vb = COL_Q // ATTN_WIDTH, COL_K // KV_WIDTH, COL_V // KV_WIDTH
    return pl.pallas_call(
        _attn_prompt_body,
        out_shape=(jax.ShapeDtypeStruct((n_prompt, ATTN_WIDTH), BF16),
                   jax.ShapeDtypeStruct((TOK_BLOCK, KV_WIDTH), F32),
                   jax.ShapeDtypeStruct((TOK_BLOCK, KV_WIDTH), F32)),
        grid=(nb,),
        in_specs=[pl.BlockSpec(memory_space=pltpu.SMEM),
                  pl.BlockSpec((TOK_BLOCK, ATTN_WIDTH), lambda i: (i, qb)),
                  pl.BlockSpec((TOK_BLOCK, KV_WIDTH), lambda i: (i, kb)),
                  pl.BlockSpec((TOK_BLOCK, KV_WIDTH), lambda i: (prev(i), kb)),
                  pl.BlockSpec((TOK_BLOCK, KV_WIDTH), lambda i: (i, vb)),
                  pl.BlockSpec((TOK_BLOCK, KV_WIDTH), lambda i: (prev(i), vb)),
                  pl.BlockSpec((TOK_BLOCK, LANES), lambda i: (i, 0)),
                  pl.BlockSpec((TOK_BLOCK, LANES), lambda i: (i, 0)),
                  pl.BlockSpec((TOK_BLOCK, LANES), lambda i: (prev(i), 0)),
                  pl.BlockSpec((TOK_BLOCK, LANES), lambda i: (prev(i), 0))],
        out_specs=(pl.BlockSpec((TOK_BLOCK, ATTN_WIDTH), lambda i: (i, 0)),
                   pl.BlockSpec((TOK_BLOCK, KV_WIDTH), lambda i: (0, 0)),
                   pl.BlockSpec((TOK_BLOCK, KV_WIDTH), lambda i: (0, 0))),
        compiler_params=_params(("arbitrary",)),
        name="attn_prompt",
    )(sinks, z, z, z, z, z, cos, sin, cos, sin)


def _attn_sample_body(sink_ref, q_ref, k_ref, v_ref, cos_ref, sin_ref, ck_ref, cv_ref,
                      o_ref, ko_ref, vo_ref, *, seqs, dec):
    q = _rope(q_ref[...], cos_ref[...], sin_ref[...]) * (HEAD_DIM ** -0.5)
    kn = _rope(k_ref[...], cos_ref[...], sin_ref[...])
    vn = v_ref[...]
    rows = GQA_GROUP * dec
    tok_c = lax.broadcasted_iota(jnp.int32, (rows, WINDOW), 0) % dec
    col_c = lax.broadcasted_iota(jnp.int32, (rows, WINDOW), 1)
    mask_c = col_c > tok_c
    tok_n = lax.broadcasted_iota(jnp.int32, (rows, dec), 0) % dec
    col_n = lax.broadcasted_iota(jnp.int32, (rows, dec), 1)
    mask_n = col_n <= tok_n
    grp = lax.broadcasted_iota(jnp.int32, (rows, 1), 0) // dec
    out_rows = []
    for b in range(seqs):
        r0 = b * dec
        ck = ck_ref[b]
        cv = cv_ref[b]
        ko_ref[b, 0:WINDOW - dec, :] = ck[dec:, :]
        ko_ref[b, WINDOW - dec:WINDOW, :] = kn[r0:r0 + dec, :]
        vo_ref[b, 0:WINDOW - dec, :] = cv[dec:, :]
        vo_ref[b, WINDOW - dec:WINDOW, :] = vn[r0:r0 + dec, :]
        heads = []
        for h in range(N_KV_HEADS):
            lo = h * HEAD_DIM
            qs = jnp.concatenate(
                [q[r0:r0 + dec, (h * GQA_GROUP + g) * HEAD_DIM:(h * GQA_GROUP + g + 1) * HEAD_DIM]
                 for g in range(GQA_GROUP)], axis=0).astype(BF16)
            sink = jnp.zeros((rows, 1), F32)
            for g in range(GQA_GROUP):
                sink = jnp.where(grp == g, sink_ref[h * GQA_GROUP + g], sink)
            kch = ck[:, lo:lo + HEAD_DIM].astype(BF16)
            vch = cv[:, lo:lo + HEAD_DIM].astype(BF16)
            knh = kn[r0:r0 + dec, lo:lo + HEAD_DIM].astype(BF16)
            vnh = vn[r0:r0 + dec, lo:lo + HEAD_DIM].astype(BF16)
            dn = (((1,), (1,)), ((), ()))
            s_c = jnp.where(mask_c, lax.dot_general(qs, kch, dn, preferred_element_type=F32), NEG_INF)
            s_n = jnp.where(mask_n, lax.dot_general(qs, knh, dn, preferred_element_type=F32), NEG_INF)
            m = jnp.maximum(jnp.maximum(jnp.max(s_c, axis=1, keepdims=True),
                                        jnp.max(s_n, axis=1, keepdims=True)), sink)
            p_c = jnp.exp(s_c - m)
            p_n = jnp.exp(s_n - m)
            den = (jnp.sum(p_c, axis=1, keepdims=True) + jnp.sum(p_n, axis=1, keepdims=True)
                   + jnp.exp(sink - m))
            o = (jnp.dot(p_c.astype(BF16), vch, preferred_element_type=F32)
                 + jnp.dot(p_n.astype(BF16), vnh, preferred_element_type=F32)) / den
            heads.extend(o[g * dec:(g + 1) * dec, :] for g in range(GQA_GROUP))
        out_rows.append(jnp.concatenate(heads, axis=1))
    o_ref[...] = jnp.concatenate(out_rows, axis=0).astype(BF16)


def _attn_sample(z, cos, sin, sinks, cache_k, cache_v, n_prompt, dec_batch, dec):
    seqs = 8
    rb = seqs * dec
    base = n_prompt // rb
    qb, kb, vb = COL_Q // ATTN_WIDTH, COL_K // KV_WIDTH, COL_V // KV_WIDTH
    return pl.pallas_call(
        functools.partial(_attn_sample_body, seqs=seqs, dec=dec),
        out_shape=(jax.ShapeDtypeStruct((dec_batch * dec, ATTN_WIDTH), BF16),
                   jax.ShapeDtypeStruct((dec_batch, WINDOW, KV_WIDTH), F32),
                   jax.ShapeDtypeStruct((dec_batch, WINDOW, KV_WIDTH), F32)),
        grid=(dec_batch // seqs,),
        in_specs=[pl.BlockSpec(memory_space=pltpu.SMEM),
                  pl.BlockSpec((rb, ATTN_WIDTH), lambda i: (base + i, qb)),
                  pl.BlockSpec((rb, KV_WIDTH), lambda i: (base + i, kb)),
                  pl.BlockSpec((rb, KV_WIDTH), lambda i: (base + i, vb)),
                  pl.BlockSpec((rb, LANES), lambda i: (base + i, 0)),
                  pl.BlockSpec((rb, LANES), lambda i: (base + i, 0)),
                  pl.BlockSpec((seqs, WINDOW, KV_WIDTH), lambda i: (i, 0, 0)),
                  pl.BlockSpec((seqs, WINDOW, KV_WIDTH), lambda i: (i, 0, 0))],
        out_specs=(pl.BlockSpec((rb, ATTN_WIDTH), lambda i: (i, 0)),
                   pl.BlockSpec((seqs, WINDOW, KV_WIDTH), lambda i: (i, 0, 0)),
                   pl.BlockSpec((seqs, WINDOW, KV_WIDTH), lambda i: (i, 0, 0))),
        compiler_params=_params(("arbitrary",)),
        name="attn_sample",
    )(sinks, z, z, z, cos, sin, cache_k, cache_v)


def _gmlp_body(u_ref, v_ref, lg_ref, lb_ref, ws_ref, bs_ref, gm_ref, vs_ref, *, n_prompt_blocks, dec):
    i = pl.program_id(0)
    u = _gelu(u_ref[...])
    v = _gelu(v_ref[...])
    mu = jnp.mean(v, axis=-1, keepdims=True)
    var = jnp.mean(jnp.square(v - mu), axis=-1, keepdims=True)
    v = (v - mu) * lax.rsqrt(var + EPS) * lg_ref[...] + lb_ref[...]
    vs_ref[...] = v
    row = lax.broadcasted_iota(jnp.int32, (GM_CHUNK, GM_CHUNK), 0)
    col = lax.broadcasted_iota(jnp.int32, (GM_CHUNK, GM_CHUNK), 1)
    shift = jnp.where(i < n_prompt_blocks, GM_CHUNK.bit_length() - 1, dec.bit_length() - 1)
    mask = (col <= row) & ((row >> shift) == (col >> shift))
    vb = v.astype(BF16)
    outs = []
    for g in range(GM_GROUPS):
        w = jnp.where(mask, ws_ref[0, g], 0.0).astype(BF16)
        lo = g * GM_GROUP_DIM
        mix = jnp.dot(w, vb[:, lo:lo + GM_GROUP_DIM], preferred_element_type=F32) + bs_ref[0, g]
        outs.append(u[:, lo:lo + GM_GROUP_DIM] * mix)
    gm_ref[...] = jnp.concatenate(outs, axis=1).astype(BF16)


def _gmlp(z, ln_g, ln_b, w_mix, b_mix, n_tokens, n_prompt, dec):
    nb = n_tokens // TOK_BLOCK
    npb = n_prompt // TOK_BLOCK
    ub, vb = COL_U // GM_WIDTH, COL_VG // GM_WIDTH
    variant = lambda i: jnp.minimum(i // npb, 1)
    return pl.pallas_call(
        functools.partial(_gmlp_body, n_prompt_blocks=npb, dec=dec),
        out_shape=(jax.ShapeDtypeStruct((n_tokens, GM_WIDTH), BF16),
                   jax.ShapeDtypeStruct(((nb - npb + 1) * TOK_BLOCK, GM_WIDTH), F32)),
        grid=(nb,),
        in_specs=[pl.BlockSpec((TOK_BLOCK, GM_WIDTH), lambda i: (i, ub)),
                  pl.BlockSpec((TOK_BLOCK, GM_WIDTH), lambda i: (i, vb)),
                  pl.BlockSpec((1, GM_WIDTH), lambda i: (0, 0)),
                  pl.BlockSpec((1, GM_WIDTH), lambda i: (0, 0)),
                  pl.BlockSpec((1, GM_GROUPS, GM_CHUNK, GM_CHUNK), lambda i: (variant(i), 0, 0, 0)),
                  pl.BlockSpec((1, GM_GROUPS, GM_CHUNK, 1), lambda i: (variant(i), 0, 0, 0))],
        out_specs=(pl.BlockSpec((TOK_BLOCK, GM_WIDTH), lambda i: (i, 0)),
                   pl.BlockSpec((TOK_BLOCK, GM_WIDTH), lambda i: (jnp.maximum(i - (npb - 1), 0), 0))),
        compiler_params=_params(("arbitrary",)),
        name="gmlp",
    )(z, z, ln_g, ln_b, w_mix, b_mix)


def _merge_body(attn_ref, gm_ref, ga_ref, gb_ref, x_ref, woa_ref, wog_ref, wout_ref, gf_ref,
                h_ref, hn_ref):
    a = jnp.dot(attn_ref[...], woa_ref[...], preferred_element_type=F32)
    b = jnp.dot(gm_ref[...], wog_ref[...], preferred_element_type=F32)
    merged = jax.nn.sigmoid(ga_ref[...]) * a + jax.nn.sigmoid(gb_ref[...]) * b
    h = x_ref[...] + jnp.dot(merged.astype(BF16), wout_ref[...], preferred_element_type=F32)
    h_ref[...] = h
    hn_ref[...] = _rms_rows(h, gf_ref[...])


def _merge(attn, gm, z, x, w_oa, w_og, w_out, g_ffn, tm):
    n = x.shape[0]
    once = pl.Buffered(1)
    return pl.pallas_call(
        _merge_body,
        out_shape=(jax.ShapeDtypeStruct((n, D_MODEL), F32), jax.ShapeDtypeStruct((n, D_MODEL), F32)),
        grid=(n // tm,),
        in_specs=[pl.BlockSpec((tm, ATTN_WIDTH), lambda i: (i, 0)),
                  pl.BlockSpec((tm, GM_WIDTH), lambda i: (i, 0)),
                  pl.BlockSpec((tm, D_MODEL), lambda i: (i, COL_GA // D_MODEL)),
                  pl.BlockSpec((tm, D_MODEL), lambda i: (i, COL_GB // D_MODEL)),
                  pl.BlockSpec((tm, D_MODEL), lambda i: (i, 0)),
                  pl.BlockSpec((ATTN_WIDTH, D_MODEL), lambda i: (0, 0), pipeline_mode=once),
                  pl.BlockSpec((GM_WIDTH, D_MODEL), lambda i: (0, 0), pipeline_mode=once),
                  pl.BlockSpec((D_MODEL, D_MODEL), lambda i: (0, 0), pipeline_mode=once),
                  pl.BlockSpec((1, D_MODEL), lambda i: (0, 0))],
        out_specs=(pl.BlockSpec((tm, D_MODEL), lambda i: (i, 0)),
                   pl.BlockSpec((tm, D_MODEL), lambda i: (i, 0))),
        compiler_params=_params(("arbitrary",)),
        name="merge",
    )(attn, gm, z, z, x, w_oa, w_og, w_out, g_ffn)


def _topk_rows(s, k):
    n = s.shape[0]
    rows = lax.broadcasted_iota(jnp.int32, s.shape, 0).astype(F32)
    vals, ids = [], []
    for _ in range(k):
        m = jnp.max(s, axis=0, keepdims=True)
        am = jnp.min(jnp.where(s == m, rows, float(n)), axis=0, keepdims=True)
        vals.append(m)
        ids.append(am)
        s = jnp.where(rows == am, -jnp.inf, s)
    return jnp.concatenate(vals, axis=0), jnp.concatenate(ids, axis=0)


def _take_rows(table, sel, n):
    out = jnp.zeros(sel.shape, table.dtype)
    for r in range(n):
        out = jnp.where(sel == r, table[r:r + 1, :], out)
    return out


_CAND_GROUPS = ([(0, 0, 8), (0, 8, 8), (1, 0, 8)]
                + [(i, 0, PEER_TOPK // (i + 1)) for i in range(2, 8)])


def _pair_candidates(v1, v2):
    sub = lax.broadcasted_iota(jnp.int32, (8, v1.shape[1]), 0)
    groups = []
    for i, j0, valid in _CAND_GROUPS:
        g = v1[i:i + 1, :] + v2[j0:j0 + 8, :]
        groups.append(g if valid == 8 else jnp.where(sub < valid, g, -jnp.inf))
    groups.append(v1[8:16, :] + v2[0:1, :])
    return jnp.concatenate(groups, axis=0)


def _candidate_ranks(row):
    grp = jnp.floor(row * 0.125)
    sub = row - 8.0 * grp
    r1 = jnp.where(grp < 2.0, 0.0, jnp.where(grp < 9.0, grp - 1.0, row - 64.0))
    r2 = jnp.where(grp < 2.0, row, jnp.where(grp < 9.0, sub, 0.0))
    return r1, r2


def _route_body(hn_ref, wq_ref, k1_ref, k2_ref, idx_ref, gate_ref):
    q = jnp.dot(hn_ref[...].astype(BF16), wq_ref[...], preferred_element_type=F32).astype(BF16)
    k1 = k1_ref[...]
    k2 = k2_ref[...]
    dn = (((1,), (1,)), ((), ()))
    idx_out, gate_out = [], []
    for h in range(PEER_HEADS):
        lo = h * 2 * PEER_HALF
        s1 = lax.dot_general(k1, q[:, lo:lo + PEER_HALF], dn, preferred_element_type=F32)
        s2 = lax.dot_general(k2, q[:, lo + PEER_HALF:lo + 2 * PEER_HALF], dn, preferred_element_type=F32)
        v1, i1 = _topk_rows(s1, PEER_TOPK)
        v2, i2 = _topk_rows(s2, PEER_TOPK)
        sc, ci = _topk_rows(_pair_candidates(v1, v2), PEER_TOPK)
        r1, r2 = _candidate_ranks(ci)
        e1 = _take_rows(i1, r1, PEER_TOPK)
        e2 = _take_rows(i2, r2, PEER_TOPK)
        idx_out.append((e1 * float(N_KEYS) + e2).astype(jnp.int32))
        p = jnp.exp(sc - sc[0:1, :])
        gate_out.append(p / jnp.sum(p, axis=0, keepdims=True))
    idx_ref[0] = jnp.concatenate(idx_out, axis=0).T
    gate_ref[0] = jnp.concatenate(gate_out, axis=0).T


def _route(hn, w_q, keys1, keys2):
    nb = hn.shape[0] // TOK_BLOCK
    return pl.pallas_call(
        _route_body,
        out_shape=(jax.ShapeDtypeStruct((nb, TOK_BLOCK, PICKS), jnp.int32),
                   jax.ShapeDtypeStruct((nb, TOK_BLOCK, PICKS), F32)),
        grid=(nb,),
        in_specs=[pl.BlockSpec((TOK_BLOCK, D_MODEL), lambda i: (i, 0)),
                  pl.BlockSpec((D_MODEL, D_MODEL), lambda i: (0, 0), pipeline_mode=pl.Buffered(1)),
                  pl.BlockSpec((N_KEYS, PEER_HALF), lambda i: (0, 0)),
                  pl.BlockSpec((N_KEYS, PEER_HALF), lambda i: (0, 0))],
        out_specs=(pl.BlockSpec((1, TOK_BLOCK, PICKS), lambda i: (i, 0, 0)),
                   pl.BlockSpec((1, TOK_BLOCK, PICKS), lambda i: (i, 0, 0))),
        compiler_params=_params(("arbitrary",)),
        name="route",
    )(hn, w_q, keys1, keys2)


GATHER_SLOTS = 4
ROW_PITCH = D_CHUNKS + 1
SLOT_ROWS = PICKS * ROW_PITCH
IDX_TAIL = 8


def _peer_body(idx_ref, gate_ref, hn_ref, h_ref, gfin_ref, tab_ref, y_ref, *scratch):
    slots = scratch[:GATHER_SLOTS]
    sems, acc_ref = scratch[GATHER_SLOTS], scratch[GATHER_SLOTS + 1]
    ahead = GATHER_SLOTS - 1
    step = pl.program_id(0)

    def issue(tok, k):
        for j in range(PICKS):
            pltpu.make_async_copy(tab_ref.at[idx_ref[0, tok, j]],
                                  slots[k].at[pl.ds(j * ROW_PITCH, D_CHUNKS), :],
                                  sems.at[k]).start(priority=j % 2)

    def wait_all(k):
        done = slots[k].at[pl.ds(0, PICKS * D_CHUNKS), :]
        pltpu.make_async_copy(done, done, sems.at[k]).wait()

    def words(k, c):
        return slots[k][pl.ds(c, PICKS, stride=ROW_PITCH), :]

    def compute(tok, k):
        x = hn_ref[pl.ds(tok, 1), :]
        acc = None
        for c in range(D_CHUNKS):
            u = lax.bitcast_convert_type(words(k, c) & jnp.int32(-65536), F32)
            term = u * x[:, c * LANES:(c + 1) * LANES]
            acc = term if acc is None else acc + term
        a = jnp.sum(acc.T, axis=0, keepdims=True)
        w = gate_ref[0, pl.ds(tok, 1), :] * _gelu(a)
        wcol = jnp.broadcast_to(w, (LANES, PICKS)).T
        outs = []
        for c in range(D_CHUNKS):
            v = lax.bitcast_convert_type(words(k, c) << 16, F32)
            outs.append(jnp.sum(wcol * v, axis=0, keepdims=True))
        acc_ref[pl.ds(tok, 1), :] = jnp.concatenate(outs, axis=1)

    @pl.when(step == 0)
    def _():
        for t in range(ahead):
            issue(t, t)

    def group(q, carry):
        for k in range(GATHER_SLOTS):
            tok = q * GATHER_SLOTS + k
            wait_all(k)
            issue(tok + ahead, (k + ahead) % GATHER_SLOTS)
            compute(tok, k)
        return carry

    lax.fori_loop(0, TOK_BLOCK // GATHER_SLOTS, group, 0)

    @pl.when(step == pl.num_programs(0) - 1)
    def _():
        for t in range(ahead):
            wait_all(t)

    y_ref[...] = _rms_rows(h_ref[...] + acc_ref[...], gfin_ref[...])


def _peer(idx, gate, hn, h, g_final, table):
    n = hn.shape[0]
    nb = n // TOK_BLOCK
    assert TOK_BLOCK % GATHER_SLOTS == 0 and GATHER_SLOTS - 1 <= IDX_TAIL
    idx = jnp.concatenate([idx, jnp.roll(idx[:, :IDX_TAIL, :], -1, axis=0)], axis=1)
    return pl.pallas_call(
        _peer_body,
        out_shape=jax.ShapeDtypeStruct((n, D_MODEL), F32),
        grid=(nb,),
        in_specs=[pl.BlockSpec((1, TOK_BLOCK + IDX_TAIL, PICKS), lambda i: (i, 0, 0), memory_space=pltpu.SMEM),
                  pl.BlockSpec((1, TOK_BLOCK, PICKS), lambda i: (i, 0, 0)),
                  pl.BlockSpec((TOK_BLOCK, D_MODEL), lambda i: (i, 0)),
                  pl.BlockSpec((TOK_BLOCK, D_MODEL), lambda i: (i, 0)),
                  pl.BlockSpec((1, D_MODEL), lambda i: (0, 0)),
                  pl.BlockSpec(memory_space=pl.ANY)],
        out_specs=pl.BlockSpec((TOK_BLOCK, D_MODEL), lambda i: (i, 0)),
        scratch_shapes=([pltpu.VMEM((SLOT_ROWS, LANES), jnp.int32) for _ in range(GATHER_SLOTS)]
                        + [pltpu.SemaphoreType.DMA((GATHER_SLOTS,)),
                           pltpu.VMEM((TOK_BLOCK, D_MODEL), F32)]),
        compiler_params=_params(("arbitrary",)),
        name="peer",
    )(idx, gate, hn, h, g_final, table)


def _pack_body(u_ref, v_ref, o_ref):
    for c in range(D_CHUNKS):
        sl = slice(c * LANES, (c + 1) * LANES)
        ub = lax.bitcast_convert_type(u_ref[:, sl].astype(BF16).astype(F32), jnp.int32)
        vb = lax.bitcast_convert_type(v_ref[:, sl].astype(BF16).astype(F32), jnp.int32)
        o_ref[:, c, :] = ub | lax.shift_right_logical(vb, 16)


def _pack_expert_table(peer_u, peer_v):
    n_exp = peer_u.shape[0]
    rows = 256
    return pl.pallas_call(
        _pack_body,
        out_shape=jax.ShapeDtypeStruct((n_exp, D_CHUNKS, LANES), jnp.int32),
        grid=(n_exp // rows,),
        in_specs=[pl.BlockSpec((rows, D_MODEL), lambda i: (i, 0)),
                  pl.BlockSpec((rows, D_MODEL), lambda i: (i, 0))],
        out_specs=pl.BlockSpec((rows, D_CHUNKS, LANES), lambda i: (i, 0, 0)),
        compiler_params=_params(("arbitrary",)),
        name="pack_table",
    )(peer_u, peer_v)


def _rope_tables(n_prompt, n_sample, dec):
    half = HEAD_DIM // 2
    inv = 1.0 / (ROPE_THETA ** (jnp.arange(half, dtype=F32) * (2.0 / HEAD_DIM)))
    pos = jnp.concatenate([jnp.arange(n_prompt), PAST_LEN + jnp.arange(n_sample) % dec])
    ang = pos.astype(F32)[:, None] * inv[None, :]
    c, s = jnp.cos(ang), jnp.sin(ang)
    return jnp.concatenate([c, c, c, c], axis=1), jnp.concatenate([-s, s, -s, s], axis=1)


def _row_tile(n, cap):
    t = cap
    while n % t:
        t //= 2
    return t


def kernel(x_prompt, x_sample, cache_k, cache_v, g_mix, w_in, attn_sinks, gm_ln_g, gm_ln_b,
           gm_w_s, gm_b_s, w_o_attn, w_o_gm, w_out, g_ffn, peer_w_q, peer_keys1, peer_keys2,
           peer_u, peer_v, g_final):
    depth = g_mix.shape[0]
    assert depth == 1 and x_prompt.shape[0] == 1
    n_prompt = x_prompt.shape[1]
    dec_batch, dec = x_sample.shape[0], x_sample.shape[1]
    n_sample = dec_batch * dec
    n = n_prompt + n_sample
    assert n_prompt % TOK_BLOCK == 0 and n_sample % TOK_BLOCK == 0 and TOK_BLOCK % dec == 0
    assert cache_k.shape[2] == WINDOW and dec_batch % 8 == 0

    x = jnp.concatenate([x_prompt[0], x_sample.reshape(n_sample, D_MODEL)], axis=0)
    wi = w_in[0]
    a, kv, gmw = ATTN_WIDTH, KV_WIDTH, GM_WIDTH
    w_perm = jnp.concatenate([wi[:, a + 2 * kv + 2 * gmw:], wi[:, :a], wi[:, a + 2 * kv:a + 2 * kv + 2 * gmw],
                              wi[:, a:a + 2 * kv]], axis=1).astype(BF16)
    cos, sin = _rope_tables(n_prompt, n_sample, dec)
    reps = TOK_BLOCK // dec
    w_mix = jnp.stack([gm_w_s[0], jnp.tile(gm_w_s[0][:, :dec, :dec], (1, reps, reps))])
    b_mix = jnp.stack([gm_b_s[0], jnp.tile(gm_b_s[0][:, :dec], (1, reps))])[..., None]
    table = _pack_expert_table(peer_u[0], peer_v[0])

    z = _inproj(x, g_mix, w_perm, _row_tile(n, 512), 512)
    attn_p, k_p, v_p = _attn_prompt(z, cos, sin, attn_sinks[0], n_prompt)
    ck = cache_k[0].reshape(dec_batch, WINDOW, KV_WIDTH)
    cv = cache_v[0].reshape(dec_batch, WINDOW, KV_WIDTH)
    attn_s, k_s, v_s = _attn_sample(z, cos, sin, attn_sinks[0], ck, cv, n_prompt, dec_batch, dec)
    attn = jnp.concatenate([attn_p, attn_s], axis=0)
    gm, v_state = _gmlp(z, gm_ln_g, gm_ln_b, w_mix, b_mix, n, n_prompt, dec)
    h, hn = _merge(attn, gm, z, x, w_o_attn[0].astype(BF16), w_o_gm[0].astype(BF16),
                   w_out[0].astype(BF16), g_ffn, _row_tile(n, 256))
    idx, gate = _route(hn, peer_w_q[0].astype(BF16), peer_keys1[0].astype(BF16),
                       peer_keys2[0].astype(BF16))
    y = _peer(idx, gate, hn, h, g_final.reshape(1, D_MODEL), table)

    kv_shape = (1, 1, WINDOW, N_KV_HEADS, HEAD_DIM)
    skv_shape = (1, dec_batch, WINDOW, N_KV_HEADS, HEAD_DIM)
    return (y[:n_prompt].reshape(1, n_prompt, D_MODEL),
            y[n_prompt:].reshape(dec_batch, dec, D_MODEL),
            k_p.reshape(kv_shape), v_p.reshape(kv_shape),
            k_s.reshape(skv_shape), v_s.reshape(skv_shape),
            v_state[:TOK_BLOCK].reshape(1, 1, GM_CHUNK, GM_WIDTH),
            v_state[TOK_BLOCK:].reshape(1, dec_batch, dec, GM_WIDTH))
```

```python
import functools

import jax
import jax.numpy as jnp
from jax import lax
from jax.experimental import pallas as pl
from jax.experimental.pallas import tpu as pltpu

F32 = jnp.float32
BF16 = jnp.bfloat16

D_MODEL = 2048
HEAD_DIM = 64
N_Q_HEADS = 16
N_KV_HEADS = 4
GQA_GROUP = N_Q_HEADS // N_KV_HEADS
ATTN_WIDTH = N_Q_HEADS * HEAD_DIM
KV_WIDTH = N_KV_HEADS * HEAD_DIM
WINDOW = 128
GM_WIDTH = D_MODEL // 2
GM_GROUPS = 4
GM_GROUP_DIM = GM_WIDTH // GM_GROUPS
GM_CHUNK = 128
N_KEYS = 128
PEER_HEADS = 8
PEER_TOPK = 16
PEER_HALF = 128
PICKS = PEER_HEADS * PEER_TOPK
ROPE_THETA = 10000.0
PAST_LEN = 16384
EPS = 1e-6
NEG_INF = -1e30

LANES = 128
D_CHUNKS = D_MODEL // LANES
TOK_BLOCK = 128
VMEM_LIMIT = 56 * 1024 * 1024

COL_GA, COL_GB = 0, D_MODEL
COL_Q = 2 * D_MODEL
COL_U = COL_Q + ATTN_WIDTH
COL_VG = COL_U + GM_WIDTH
COL_K = COL_VG + GM_WIDTH
COL_V = COL_K + KV_WIDTH
IN_WIDTH = COL_V + KV_WIDTH


def _params(sem, **kw):
    return pltpu.CompilerParams(dimension_semantics=sem, vmem_limit_bytes=VMEM_LIMIT, **kw)


def _gelu(x):
    return 0.5 * x * (1.0 + lax.erf(x * 0.7071067811865476))


def _rms_rows(x, g):
    return x * lax.rsqrt(jnp.mean(x * x, axis=-1, keepdims=True) + EPS) * g


def _rope(x, cos, sin):
    width = x.shape[1]
    lane = lax.broadcasted_iota(jnp.int32, x.shape, 1)
    first_half = (lane & (HEAD_DIM // 2)) == 0
    partner = jnp.where(first_half,
                        pltpu.roll(x, width - HEAD_DIM // 2, axis=1),
                        pltpu.roll(x, HEAD_DIM // 2, axis=1))
    reps = width // LANES
    if reps > 1:
        cos = jnp.concatenate([cos] * reps, axis=1)
        sin = jnp.concatenate([sin] * reps, axis=1)
    return x * cos + partner * sin


def _inproj_body(x_ref, g_ref, w_ref, z_ref, xn_ref):
    @pl.when(pl.program_id(1) == 0)
    def _():
        xn_ref[...] = _rms_rows(x_ref[...], g_ref[...]).astype(BF16)

    z_ref[...] = jnp.dot(xn_ref[...], w_ref[...], preferred_element_type=F32)


def _inproj(x, g, w, tm, tn):
    n = x.shape[0]
    return pl.pallas_call(
        _inproj_body,
        out_shape=jax.ShapeDtypeStruct((n, IN_WIDTH), F32),
        grid=(n // tm, IN_WIDTH // tn),
        in_specs=[pl.BlockSpec((tm, D_MODEL), lambda i, j: (i, 0)),
                  pl.BlockSpec((1, D_MODEL), lambda i, j: (0, 0)),
                  pl.BlockSpec((D_MODEL, tn), lambda i, j: (0, j))],
        out_specs=pl.BlockSpec((tm, tn), lambda i, j: (i, j)),
        scratch_shapes=[pltpu.VMEM((tm, D_MODEL), BF16)],
        compiler_params=_params(("arbitrary", "arbitrary")),
        name="inproj",
    )(x, g, w)


def _softmax_sink_pv(s, sink, v):
    m = jnp.maximum(jnp.max(s, axis=1, keepdims=True), sink)
    p = jnp.exp(s - m)
    den = jnp.sum(p, axis=1, keepdims=True) + jnp.exp(sink - m)
    return jnp.dot(p.astype(BF16), v, preferred_element_type=F32) / den


def _attn_prompt_body(sink_ref, q_ref, kc_ref, kp_ref, vc_ref, vp_ref, cc_ref, sc_ref,
                      cp_ref, sp_ref, o_ref, ko_ref, vo_ref):
    i = pl.program_id(0)
    q = (_rope(q_ref[...], cc_ref[...], sc_ref[...]) * (HEAD_DIM ** -0.5)).astype(BF16)
    kc = _rope(kc_ref[...], cc_ref[...], sc_ref[...])
    kp = _rope(kp_ref[...], cp_ref[...], sp_ref[...])
    vc = vc_ref[...]
    ko_ref[...] = kc
    vo_ref[...] = vc
    kband = jnp.concatenate([kp, kc], axis=0).astype(BF16)
    vband = jnp.concatenate([vp_ref[...], vc], axis=0).astype(BF16)
    row = lax.broadcasted_iota(jnp.int32, (TOK_BLOCK, 2 * TOK_BLOCK), 0)
    col = lax.broadcasted_iota(jnp.int32, (TOK_BLOCK, 2 * TOK_BLOCK), 1)
    mask = (col <= row + WINDOW) & (col > row) & ((i > 0) | (col >= TOK_BLOCK))
    outs = []
    for hq in range(N_Q_HEADS):
        h = hq // GQA_GROUP
        qh = q[:, hq * HEAD_DIM:(hq + 1) * HEAD_DIM]
        kh = kband[:, h * HEAD_DIM:(h + 1) * HEAD_DIM]
        vh = vband[:, h * HEAD_DIM:(h + 1) * HEAD_DIM]
        s = lax.dot_general(qh, kh, (((1,), (1,)), ((), ())), preferred_element_type=F32)
        s = jnp.where(mask, s, NEG_INF)
        outs.append(_softmax_sink_pv(s, sink_ref[hq], vh))
    o_ref[...] = jnp.concatenate(outs, axis=1).astype(BF16)


def _attn_prompt(z, cos, sin, sinks, n_prompt):
    nb = n_prompt // TOK_BLOCK
    prev = lambda i: jnp.maximum(i - 1, 0)
    qb, kb, vb = COL_Q // ATTN_WIDTH, COL_K // KV_WIDTH, COL_V // KV_WIDTH
    return pl.pallas_call(
        _attn_prompt_body,
        out_shape=(jax.ShapeDtypeStruct((n_prompt, ATTN_WIDTH), BF16),
                   jax.ShapeDtypeStruct((TOK_BLOCK, KV_WIDTH), F32),
                   jax.ShapeDtypeStruct((TOK_BLOCK, KV_WIDTH), F32)),
        grid=(nb,),
        in_specs=[pl.BlockSpec(memory_space=pltpu.SMEM),
                  pl.BlockSpec((TOK_BLOCK, ATTN_WIDTH), lambda i: (i, qb)),
                  pl.BlockSpec((TOK_BLOCK, KV_WIDTH), lambda i: (i, kb)),
                  pl.BlockSpec((TOK_BLOCK, KV_WIDTH), lambda i: (prev(i), kb)),
                  pl.BlockSpec((TOK_BLOCK, KV_WIDTH), lambda i: (i, vb)),
                  pl.BlockSpec((TOK_BLOCK, KV_WIDTH), lambda i: (prev(i), vb)),
                  pl.BlockSpec((TOK_BLOCK, LANES), lambda i: (i, 0)),
                  pl.BlockSpec((TOK_BLOCK, LANES), lambda i: (i, 0)),
                  pl.BlockSpec((TOK_BLOCK, LANES), lambda i: (prev(i), 0)),
                  pl.BlockSpec((TOK_BLOCK, LANES), lambda i: (prev(i), 0))],
        out_specs=(pl.BlockSpec((TOK_BLOCK, ATTN_WIDTH), lambda i: (i, 0)),
                   pl.BlockSpec((TOK_BLOCK, KV_WIDTH), lambda i: (0, 0)),
                   pl.BlockSpec((TOK_BLOCK, KV_WIDTH), lambda i: (0, 0))),
        compiler_params=_params(("arbitrary",)),
        name="attn_prompt",
    )(sinks, z, z, z, z, z, cos, sin, cos, sin)


def _attn_sample_body(sink_ref, q_ref, k_ref, v_ref, cos_ref, sin_ref, ck_ref, cv_ref,
                      o_ref, ko_ref, vo_ref, *, seqs, dec):
    q = _rope(q_ref[...], cos_ref[...], sin_ref[...]) * (HEAD_DIM ** -0.5)
    kn = _rope(k_ref[...], cos_ref[...], sin_ref[...])
    vn = v_ref[...]
    rows = GQA_GROUP * dec
    tok_c = lax.broadcasted_iota(jnp.int32, (rows, WINDOW), 0) % dec
    col_c = lax.broadcasted_iota(jnp.int32, (rows, WINDOW), 1)
    mask_c = col_c > tok_c
    tok_n = lax.broadcasted_iota(jnp.int32, (rows, dec), 0) % dec
    col_n = lax.broadcasted_iota(jnp.int32, (rows, dec), 1)
    mask_n = col_n <= tok_n
    grp = lax.broadcasted_iota(jnp.int32, (rows, 1), 0) // dec
    out_rows = []
    for b in range(seqs):
        r0 = b * dec
        ck = ck_ref[b]
        cv = cv_ref[b]
        ko_ref[b, 0:WINDOW - dec, :] = ck[dec:, :]
        ko_ref[b, WINDOW - dec:WINDOW, :] = kn[r0:r0 + dec, :]
        vo_ref[b, 0:WINDOW - dec, :] = cv[dec:, :]
        vo_ref[b, WINDOW - dec:WINDOW, :] = vn[r0:r0 + dec, :]
        heads = []
        for h in range(N_KV_HEADS):
            lo = h * HEAD_DIM
            qs = jnp.concatenate(
                [q[r0:r0 + dec, (h * GQA_GROUP + g) * HEAD_DIM:(h * GQA_GROUP + g + 1) * HEAD_DIM]
                 for g in range(GQA_GROUP)], axis=0).astype(BF16)
            sink = jnp.zeros((rows, 1), F32)
            for g in range(GQA_GROUP):
                sink = jnp.where(grp == g, sink_ref[h * GQA_GROUP + g], sink)
            kch = ck[:, lo:lo + HEAD_DIM].astype(BF16)
            vch = cv[:, lo:lo + HEAD_DIM].astype(BF16)
            knh = kn[r0:r0 + dec, lo:lo + HEAD_DIM].astype(BF16)
            vnh = vn[r0:r0 + dec, lo:lo + HEAD_DIM].astype(BF16)
            dn = (((1,), (1,)), ((), ()))
            s_c = jnp.where(mask_c, lax.dot_general(qs, kch, dn, preferred_element_type=F32), NEG_INF)
            s_n = jnp.where(mask_n, lax.dot_general(qs, knh, dn, preferred_element_type=F32), NEG_INF)
            m = jnp.maximum(jnp.maximum(jnp.max(s_c, axis=1, keepdims=True),
                                        jnp.max(s_n, axis=1, keepdims=True)), sink)
            p_c = jnp.exp(s_c - m)
            p_n = jnp.exp(s_n - m)
            den = (jnp.sum(p_c, axis=1, keepdims=True) + jnp.sum(p_n, axis=1, keepdims=True)
                   + jnp.exp(sink - m))
            o = (jnp.dot(p_c.astype(BF16), vch, preferred_element_type=F32)
                 + jnp.dot(p_n.astype(BF16), vnh, preferred_element_type=F32)) / den
            heads.extend(o[g * dec:(g + 1) * dec, :] for g in range(GQA_GROUP))
        out_rows.append(jnp.concatenate(heads, axis=1))
    o_ref[...] = jnp.concatenate(out_rows, axis=0).astype(BF16)


def _attn_sample(z, cos, sin, sinks, cache_k, cache_v, n_prompt, dec_batch, dec):
    seqs = 8
    rb = seqs * dec
    base = n_prompt // rb
    qb, kb, vb = COL_Q // ATTN_WIDTH, COL_K // KV_WIDTH, COL_V // KV_WIDTH
    return pl.pallas_call(
        functools.partial(_attn_sample_body, seqs=seqs, dec=dec),
        out_shape=(jax.ShapeDtypeStruct((dec_batch * dec, ATTN_WIDTH), BF16),
                   jax.ShapeDtypeStruct((dec_batch, WINDOW, KV_WIDTH), F32),
                   jax.ShapeDtypeStruct((dec_batch, WINDOW, KV_WIDTH), F32)),
        grid=(dec_batch // seqs,),
        in_specs=[pl.BlockSpec(memory_space=pltpu.SMEM),
                  pl.BlockSpec((rb, ATTN_WIDTH), lambda i: (base + i, qb)),
                  pl.BlockSpec((rb, KV_WIDTH), lambda i: (base + i, kb)),
                  pl.BlockSpec((rb, KV_WIDTH), lambda i: (base + i, vb)),
                  pl.BlockSpec((rb, LANES), lambda i: (base + i, 0)),
                  pl.BlockSpec((rb, LANES), lambda i: (base + i, 0)),
                  pl.BlockSpec((seqs, WINDOW, KV_WIDTH), lambda i: (i, 0, 0)),
                  pl.BlockSpec((seqs, WINDOW, KV_WIDTH), lambda i: (i, 0, 0))],
        out_specs=(pl.BlockSpec((rb, ATTN_WIDTH), lambda i: (i, 0)),
                   pl.BlockSpec((seqs, WINDOW, KV_WIDTH), lambda i: (i, 0, 0)),
                   pl.BlockSpec((seqs, WINDOW, KV_WIDTH), lambda i: (i, 0, 0))),
        compiler_params=_params(("arbitrary",)),
        name="attn_sample",
    )(sinks, z, z, z, cos, sin, cache_k, cache_v)


def _gmlp_body(u_ref, v_ref, lg_ref, lb_ref, ws_ref, bs_ref, gm_ref, vs_ref, *, n_prompt_blocks, dec):
    i = pl.program_id(0)
    u = _gelu(u_ref[...])
    v = _gelu(v_ref[...])
    mu = jnp.mean(v, axis=-1, keepdims=True)
    var = jnp.mean(jnp.square(v - mu), axis=-1, keepdims=True)
    v = (v - mu) * lax.rsqrt(var + EPS) * lg_ref[...] + lb_ref[...]
    vs_ref[...] = v
    row = lax.broadcasted_iota(jnp.int32, (GM_CHUNK, GM_CHUNK), 0)
    col = lax.broadcasted_iota(jnp.int32, (GM_CHUNK, GM_CHUNK), 1)
    shift = jnp.where(i < n_prompt_blocks, GM_CHUNK.bit_length() - 1, dec.bit_length() - 1)
    mask = (col <= row) & ((row >> shift) == (col >> shift))
    vb = v.astype(BF16)
    outs = []
    for g in range(GM_GROUPS):
        w = jnp.where(mask, ws_ref[0, g], 0.0).astype(BF16)
        lo = g * GM_GROUP_DIM
        mix = jnp.dot(w, vb[:, lo:lo + GM_GROUP_DIM], preferred_element_type=F32) + bs_ref[0, g]
        outs.append(u[:, lo:lo + GM_GROUP_DIM] * mix)
    gm_ref[...] = jnp.concatenate(outs, axis=1).astype(BF16)


def _gmlp(z, ln_g, ln_b, w_mix, b_mix, n_tokens, n_prompt, dec):
    nb = n_tokens // TOK_BLOCK
    npb = n_prompt // TOK_BLOCK
    ub, vb = COL_U // GM_WIDTH, COL_VG // GM_WIDTH
    variant = lambda i: jnp.minimum(i // npb, 1)
    return pl.pallas_call(
        functools.partial(_gmlp_body, n_prompt_blocks=npb, dec=dec),
        out_shape=(jax.ShapeDtypeStruct((n_tokens, GM_WIDTH), BF16),
                   jax.ShapeDtypeStruct(((nb - npb + 1) * TOK_BLOCK, GM_WIDTH), F32)),
        grid=(nb,),
        in_specs=[pl.BlockSpec((TOK_BLOCK, GM_WIDTH), lambda i: (i, ub)),
                  pl.BlockSpec((TOK_BLOCK, GM_WIDTH), lambda i: (i, vb)),
                  pl.BlockSpec((1, GM_WIDTH), lambda i: (0, 0)),
                  pl.BlockSpec((1, GM_WIDTH), lambda i: (0, 0)),
                  pl.BlockSpec((1, GM_GROUPS, GM_CHUNK, GM_CHUNK), lambda i: (variant(i), 0, 0, 0)),
                  pl.BlockSpec((1, GM_GROUPS, GM_CHUNK, 1), lambda i: (variant(i), 0, 0, 0))],
        out_specs=(pl.BlockSpec((TOK_BLOCK, GM_WIDTH), lambda i: (i, 0)),
                   pl.BlockSpec((TOK_BLOCK, GM_WIDTH), lambda i: (jnp.maximum(i - (npb - 1), 0), 0))),
        compiler_params=_params(("arbitrary",)),
        name="gmlp",
    )(z, z, ln_g, ln_b, w_mix, b_mix)


def _merge_body(attn_ref, gm_ref, ga_ref, gb_ref, x_ref, woa_ref, wog_ref, wout_ref, gf_ref,
                h_ref, hn_ref):
    a = jnp.dot(attn_ref[...], woa_ref[...], preferred_element_type=F32)
    b = jnp.dot(gm_ref[...], wog_ref[...], preferred_element_type=F32)
    merged = jax.nn.sigmoid(ga_ref[...]) * a + jax.nn.sigmoid(gb_ref[...]) * b
    h = x_ref[...] + jnp.dot(merged.astype(BF16), wout_ref[...], preferred_element_type=F32)
    h_ref[...] = h
    hn_ref[...] = _rms_rows(h, gf_ref[...])


def _merge(attn, gm, z, x, w_oa, w_og, w_out, g_ffn, tm):
    n = x.shape[0]
    once = pl.Buffered(1)
    return pl.pallas_call(
        _merge_body,
        out_shape=(jax.ShapeDtypeStruct((n, D_MODEL), F32), jax.ShapeDtypeStruct((n, D_MODEL), F32)),
        grid=(n // tm,),
        in_specs=[pl.BlockSpec((tm, ATTN_WIDTH), lambda i: (i, 0)),
                  pl.BlockSpec((tm, GM_WIDTH), lambda i: (i, 0)),
                  pl.BlockSpec((tm, D_MODEL), lambda i: (i, COL_GA // D_MODEL)),
                  pl.BlockSpec((tm, D_MODEL), lambda i: (i, COL_GB // D_MODEL)),
                  pl.BlockSpec((tm, D_MODEL), lambda i: (i, 0)),
                  pl.BlockSpec((ATTN_WIDTH, D_MODEL), lambda i: (0, 0), pipeline_mode=once),
                  pl.BlockSpec((GM_WIDTH, D_MODEL), lambda i: (0, 0), pipeline_mode=once),
                  pl.BlockSpec((D_MODEL, D_MODEL), lambda i: (0, 0), pipeline_mode=once),
                  pl.BlockSpec((1, D_MODEL), lambda i: (0, 0))],
        out_specs=(pl.BlockSpec((tm, D_MODEL), lambda i: (i, 0)),
                   pl.BlockSpec((tm, D_MODEL), lambda i: (i, 0))),
        compiler_params=_params(("arbitrary",)),
        name="merge",
    )(attn, gm, z, z, x, w_oa, w_og, w_out, g_ffn)


def _topk_rows(s, k):
    n = s.shape[0]
    rows = lax.broadcasted_iota(jnp.int32, s.shape, 0).astype(F32)
    vals, ids = [], []
    for _ in range(k):
        m = jnp.max(s, axis=0, keepdims=True)
        am = jnp.min(jnp.where(s == m, rows, float(n)), axis=0, keepdims=True)
        vals.append(m)
        ids.append(am)
        s = jnp.where(rows == am, -jnp.inf, s)
    return jnp.concatenate(vals, axis=0), jnp.concatenate(ids, axis=0)


def _take_rows(table, sel, n):
    out = jnp.zeros(sel.shape, table.dtype)
    for r in range(n):
        out = jnp.where(sel == r, table[r:r + 1, :], out)
    return out


_CAND_GROUPS = ([(0, 0, 8), (0, 8, 8), (1, 0, 8)]
                + [(i, 0, PEER_TOPK // (i + 1)) for i in range(2, 8)])


def _pair_candidates(v1, v2):
    sub = lax.broadcasted_iota(jnp.int32, (8, v1.shape[1]), 0)
    groups = []
    for i, j0, valid in _CAND_GROUPS:
        g = v1[i:i + 1, :] + v2[j0:j0 + 8, :]
        groups.append(g if valid == 8 else jnp.where(sub < valid, g, -jnp.inf))
    groups.append(v1[8:16, :] + v2[0:1, :])
    return jnp.concatenate(groups, axis=0)


def _candidate_ranks(row):
    grp = jnp.floor(row * 0.125)
    sub = row - 8.0 * grp
    r1 = jnp.where(grp < 2.0, 0.0, jnp.where(grp < 9.0, grp - 1.0, row - 64.0))
    r2 = jnp.where(grp < 2.0, row, jnp.where(grp < 9.0, sub, 0.0))
    return r1, r2


def _route_body(hn_ref, wq_ref, k1_ref, k2_ref, idx_ref, gate_ref):
    q = jnp.dot(hn_ref[...].astype(BF16), wq_ref[...], preferred_element_type=F32).astype(BF16)
    k1 = k1_ref[...]
    k2 = k2_ref[...]
    dn = (((1,), (1,)), ((), ()))
    idx_out, gate_out = [], []
    for h in range(PEER_HEADS):
        lo = h * 2 * PEER_HALF
        s1 = lax.dot_general(k1, q[:, lo:lo + PEER_HALF], dn, preferred_element_type=F32)
        s2 = lax.dot_general(k2, q[:, lo + PEER_HALF:lo + 2 * PEER_HALF], dn, preferred_element_type=F32)
        v1, i1 = _topk_rows(s1, PEER_TOPK)
        v2, i2 = _topk_rows(s2, PEER_TOPK)
        sc, ci = _topk_rows(_pair_candidates(v1, v2), PEER_TOPK)
        r1, r2 = _candidate_ranks(ci)
        e1 = _take_rows(i1, r1, PEER_TOPK)
        e2 = _take_rows(i2, r2, PEER_TOPK)
        idx_out.append((e1 * float(N_KEYS) + e2).astype(jnp.int32))
        p = jnp.exp(sc - sc[0:1, :])
        gate_out.append(p / jnp.sum(p, axis=0, keepdims=True))
    idx_ref[0] = jnp.concatenate(idx_out, axis=0).T
    gate_ref[0] = jnp.concatenate(gate_out, axis=0).T


def _route(hn, w_q, keys1, keys2):
    nb = hn.shape[0] // TOK_BLOCK
    return pl.pallas_call(
        _route_body,
        out_shape=(jax.ShapeDtypeStruct((nb, TOK_BLOCK, PICKS), jnp.int32),
                   jax.ShapeDtypeStruct((nb, TOK_BLOCK, PICKS), F32)),
        grid=(nb,),
        in_specs=[pl.BlockSpec((TOK_BLOCK, D_MODEL), lambda i: (i, 0)),
                  pl.BlockSpec((D_MODEL, D_MODEL), lambda i: (0, 0), pipeline_mode=pl.Buffered(1)),
                  pl.BlockSpec((N_KEYS, PEER_HALF), lambda i: (0, 0)),
                  pl.BlockSpec((N_KEYS, PEER_HALF), lambda i: (0, 0))],
        out_specs=(pl.BlockSpec((1, TOK_BLOCK, PICKS), lambda i: (i, 0, 0)),
                   pl.BlockSpec((1, TOK_BLOCK, PICKS), lambda i: (i, 0, 0))),
        compiler_params=_params(("arbitrary",)),
        name="route",
    )(hn, w_q, keys1, keys2)


GATHER_SLOTS = 8
GATHER_AHEAD = 4
ROW_PITCH = D_CHUNKS + 1
SLOT_ROWS = PICKS * ROW_PITCH
IDX_TAIL = 8
U_HALF, V_HALF = 0, 1


def _peer_body(idx_ref, gate_ref, hn_ref, h_ref, gfin_ref, tab_ref, y_ref, *scratch):
    slots = scratch[:GATHER_SLOTS]
    sems, acc_ref = scratch[GATHER_SLOTS], scratch[GATHER_SLOTS + 1]
    step = pl.program_id(0)

    def issue(tok, k):
        for j in range(PICKS):
            pltpu.make_async_copy(tab_ref.at[idx_ref[0, tok, j]],
                                  slots[k].at[pl.ds(j * ROW_PITCH, D_CHUNKS), :],
                                  sems.at[k]).start(priority=j % 2)

    def wait_all(k):
        done = slots[k].at[pl.ds(0, PICKS * D_CHUNKS), :]
        pltpu.make_async_copy(done, done, sems.at[k]).wait()

    def half(k, c, which):
        w = slots[k][pl.ds(c, PICKS, stride=ROW_PITCH), :]
        return pltpu.unpack_elementwise(w, index=which, packed_dtype=BF16, unpacked_dtype=F32)

    def mix_weights(tok, k):
        x = hn_ref[pl.ds(tok, 1), :]
        acc = None
        for c in range(D_CHUNKS):
            term = half(k, c, U_HALF) * x[:, c * LANES:(c + 1) * LANES]
            acc = term if acc is None else acc + term
        a = jnp.sum(acc.T, axis=0, keepdims=True)
        return gate_ref[0, pl.ds(tok, 1), :] * _gelu(a)

    def mix_rows(w, row, k):
        wcol = jnp.broadcast_to(w, (LANES, PICKS)).T
        outs = [jnp.sum(wcol * half(k, c, V_HALF), axis=0, keepdims=True) for c in range(D_CHUNKS)]
        acc_ref[pl.ds(row, 1), :] = jnp.concatenate(outs, axis=1)

    @pl.when(step == 0)
    def _():
        slots[GATHER_SLOTS - 1][...] = jnp.zeros((SLOT_ROWS, LANES), jnp.uint32)
        for t in range(GATHER_AHEAD):
            issue(t, t)

    def group(q, w_prev):
        for k in range(GATHER_SLOTS):
            tok = q * GATHER_SLOTS + k
            wait_all(k)
            issue(tok + GATHER_AHEAD, (k + GATHER_AHEAD) % GATHER_SLOTS)
            mix_rows(w_prev, tok + 7, (k - 1) % GATHER_SLOTS)
            w_prev = mix_weights(tok, k)
        return w_prev

    w_last = lax.fori_loop(0, TOK_BLOCK // GATHER_SLOTS, group, jnp.zeros((1, PICKS), F32))
    mix_rows(w_last, TOK_BLOCK + 7, GATHER_SLOTS - 1)

    @pl.when(step == pl.num_programs(0) - 1)
    def _():
        for t in range(GATHER_AHEAD):
            wait_all(t)

    y_ref[...] = _rms_rows(h_ref[...] + acc_ref[8:, :], gfin_ref[...])


def _peer(idx, gate, hn, h, g_final, table):
    n = hn.shape[0]
    nb = n // TOK_BLOCK
    assert TOK_BLOCK % GATHER_SLOTS == 0 and GATHER_AHEAD <= IDX_TAIL and GATHER_AHEAD < GATHER_SLOTS - 1
    idx = jnp.concatenate([idx, jnp.roll(idx[:, :IDX_TAIL, :], -1, axis=0)], axis=1)
    return pl.pallas_call(
        _peer_body,
        out_shape=jax.ShapeDtypeStruct((n, D_MODEL), F32),
        grid=(nb,),
        in_specs=[pl.BlockSpec((1, TOK_BLOCK + IDX_TAIL, PICKS), lambda i: (i, 0, 0), memory_space=pltpu.SMEM),
                  pl.BlockSpec((1, TOK_BLOCK, PICKS), lambda i: (i, 0, 0)),
                  pl.BlockSpec((TOK_BLOCK, D_MODEL), lambda i: (i, 0)),
                  pl.BlockSpec((TOK_BLOCK, D_MODEL), lambda i: (i, 0)),
                  pl.BlockSpec((1, D_MODEL), lambda i: (0, 0)),
                  pl.BlockSpec(memory_space=pl.ANY)],
        out_specs=pl.BlockSpec((TOK_BLOCK, D_MODEL), lambda i: (i, 0)),
        scratch_shapes=([pltpu.VMEM((SLOT_ROWS, LANES), jnp.uint32) for _ in range(GATHER_SLOTS)]
                        + [pltpu.SemaphoreType.DMA((GATHER_SLOTS,)),
                           pltpu.VMEM((8 + TOK_BLOCK, D_MODEL), F32)]),
        compiler_params=_params(("arbitrary",)),
        name="peer",
    )(idx, gate, hn, h, g_final, table)


def _pack_body(u_ref, v_ref, o_ref):
    for c in range(D_CHUNKS):
        sl = slice(c * LANES, (c + 1) * LANES)
        halves = {U_HALF: u_ref[:, sl], V_HALF: v_ref[:, sl]}
        o_ref[:, c, :] = pltpu.pack_elementwise([halves[0], halves[1]], packed_dtype=BF16)


def _pack_expert_table(peer_u, peer_v):
    n_exp = peer_u.shape[0]
    rows = 256
    return pl.pallas_call(
        _pack_body,
        out_shape=jax.ShapeDtypeStruct((n_exp, D_CHUNKS, LANES), jnp.uint32),
        grid=(n_exp // rows,),
        in_specs=[pl.BlockSpec((rows, D_MODEL), lambda i: (i, 0)),
                  pl.BlockSpec((rows, D_MODEL), lambda i: (i, 0))],
        out_specs=pl.BlockSpec((rows, D_CHUNKS, LANES), lambda i: (i, 0, 0)),
        compiler_params=_params(("arbitrary",)),
        name="pack_table",
    )(peer_u, peer_v)


def _rope_tables(n_prompt, n_sample, dec):
    half = HEAD_DIM // 2
    inv = 1.0 / (ROPE_THETA ** (jnp.arange(half, dtype=F32) * (2.0 / HEAD_DIM)))
    pos = jnp.concatenate([jnp.arange(n_prompt), PAST_LEN + jnp.arange(n_sample) % dec])
    ang = pos.astype(F32)[:, None] * inv[None, :]
    c, s = jnp.cos(ang), jnp.sin(ang)
    return jnp.concatenate([c, c, c, c], axis=1), jnp.concatenate([-s, s, -s, s], axis=1)


def _row_tile(n, cap):
    return max(t for t in range(TOK_BLOCK, cap + 1, TOK_BLOCK) if n % t == 0)


def kernel(x_prompt, x_sample, cache_k, cache_v, g_mix, w_in, attn_sinks, gm_ln_g, gm_ln_b,
           gm_w_s, gm_b_s, w_o_attn, w_o_gm, w_out, g_ffn, peer_w_q, peer_keys1, peer_keys2,
           peer_u, peer_v, g_final):
    depth = g_mix.shape[0]
    assert depth == 1 and x_prompt.shape[0] == 1
    n_prompt = x_prompt.shape[1]
    dec_batch, dec = x_sample.shape[0], x_sample.shape[1]
    n_sample = dec_batch * dec
    n = n_prompt + n_sample
    assert n_prompt % TOK_BLOCK == 0 and n_sample % TOK_BLOCK == 0 and TOK_BLOCK % dec == 0
    assert cache_k.shape[2] == WINDOW and dec_batch % 8 == 0

    x = jnp.concatenate([x_prompt[0], x_sample.reshape(n_sample, D_MODEL)], axis=0)
    wi = w_in[0]
    a, kv, gmw = ATTN_WIDTH, KV_WIDTH, GM_WIDTH
    w_perm = jnp.concatenate([wi[:, a + 2 * kv + 2 * gmw:], wi[:, :a], wi[:, a + 2 * kv:a + 2 * kv + 2 * gmw],
                              wi[:, a:a + 2 * kv]], axis=1).astype(BF16)
    cos, sin = _rope_tables(n_prompt, n_sample, dec)
    reps = TOK_BLOCK // dec
    w_mix = jnp.stack([gm_w_s[0], jnp.tile(gm_w_s[0][:, :dec, :dec], (1, reps, reps))])
    b_mix = jnp.stack([gm_b_s[0], jnp.tile(gm_b_s[0][:, :dec], (1, reps))])[..., None]
    table = _pack_expert_table(peer_u[0], peer_v[0])

    z = _inproj(x, g_mix, w_perm, _row_tile(n, 1536), 512)
    attn_p, k_p, v_p = _attn_prompt(z, cos, sin, attn_sinks[0], n_prompt)
    ck = cache_k[0].reshape(dec_batch, WINDOW, KV_WIDTH)
    cv = cache_v[0].reshape(dec_batch, WINDOW, KV_WIDTH)
    attn_s, k_s, v_s = _attn_sample(z, cos, sin, attn_sinks[0], ck, cv, n_prompt, dec_batch, dec)
    attn = jnp.concatenate([attn_p, attn_s], axis=0)
    gm, v_state = _gmlp(z, gm_ln_g, gm_ln_b, w_mix, b_mix, n, n_prompt, dec)
    h, hn = _merge(attn, gm, z, x, w_o_attn[0].astype(BF16), w_o_gm[0].astype(BF16),
                   w_out[0].astype(BF16), g_ffn, _row_tile(n, 256))
    idx, gate = _route(hn, peer_w_q[0].astype(BF16), peer_keys1[0].astype(BF16),
                       peer_keys2[0].astype(BF16))
    y = _peer(idx, gate, hn, h, g_final.reshape(1, D_MODEL), table)

    kv_shape = (1, 1, WINDOW, N_KV_HEADS, HEAD_DIM)
    skv_shape = (1, dec_batch, WINDOW, N_KV_HEADS, HEAD_DIM)
    return (y[:n_prompt].reshape(1, n_prompt, D_MODEL),
            y[n_prompt:].reshape(dec_batch, dec, D_MODEL),
            k_p.reshape(kv_shape), v_p.reshape(kv_shape),
            k_s.reshape(skv_shape), v_s.reshape(skv_shape),
            v_state[:TOK_BLOCK].reshape(1, 1, GM_CHUNK, GM_WIDTH),
            v_state[TOK_BLOCK:].reshape(1, dec_batch, dec, GM_WIDTH))
```

```python
import functools

import jax
import jax.numpy as jnp
from jax import lax
from jax.experimental import pallas as pl
from jax.experimental.pallas import tpu as pltpu

F32 = jnp.float32
BF16 = jnp.bfloat16

D_MODEL = 2048
HEAD_DIM = 64
N_Q_HEADS = 16
N_KV_HEADS = 4
GQA_GROUP = N_Q_HEADS // N_KV_HEADS
ATTN_WIDTH = N_Q_HEADS * HEAD_DIM
KV_WIDTH = N_KV_HEADS * HEAD_DIM
WINDOW = 128
GM_WIDTH = D_MODEL // 2
GM_GROUPS = 4
GM_GROUP_DIM = GM_WIDTH // GM_GROUPS
GM_CHUNK = 128
N_KEYS = 128
PEER_HEADS = 8
PEER_TOPK = 16
PEER_HALF = 128
PICKS = PEER_HEADS * PEER_TOPK
ROPE_THETA = 10000.0
PAST_LEN = 16384
EPS = 1e-6
NEG_INF = -1e30

LANES = 128
D_CHUNKS = D_MODEL // LANES
TOK_BLOCK = 128
VMEM_LIMIT = 56 * 1024 * 1024

COL_GA, COL_GB = 0, D_MODEL
COL_Q = 2 * D_MODEL
COL_U = COL_Q + ATTN_WIDTH
COL_VG = COL_U + GM_WIDTH
COL_K = COL_VG + GM_WIDTH
COL_V = COL_K + KV_WIDTH
IN_WIDTH = COL_V + KV_WIDTH


def _params(sem, **kw):
    return pltpu.CompilerParams(dimension_semantics=sem, vmem_limit_bytes=VMEM_LIMIT, **kw)


def _gelu(x):
    return 0.5 * x * (1.0 + lax.erf(x * 0.7071067811865476))


def _rms_rows(x, g):
    return x * lax.rsqrt(jnp.mean(x * x, axis=-1, keepdims=True) + EPS) * g


def _rope(x, cos, sin):
    width = x.shape[1]
    lane = lax.broadcasted_iota(jnp.int32, x.shape, 1)
    first_half = (lane & (HEAD_DIM // 2)) == 0
    partner = jnp.where(first_half,
                        pltpu.roll(x, width - HEAD_DIM // 2, axis=1),
                        pltpu.roll(x, HEAD_DIM // 2, axis=1))
    reps = width // LANES
    if reps > 1:
        cos = jnp.concatenate([cos] * reps, axis=1)
        sin = jnp.concatenate([sin] * reps, axis=1)
    return x * cos + partner * sin


def _inproj_body(x_ref, g_ref, w_ref, z_ref, xn_ref):
    @pl.when(pl.program_id(1) == 0)
    def _():
        xn_ref[...] = _rms_rows(x_ref[...], g_ref[...]).astype(BF16)

    z_ref[...] = jnp.dot(xn_ref[...], w_ref[...], preferred_element_type=F32)


def _inproj(x, g, w, tm, tn):
    n = x.shape[0]
    return pl.pallas_call(
        _inproj_body,
        out_shape=jax.ShapeDtypeStruct((n, IN_WIDTH), F32),
        grid=(n // tm, IN_WIDTH // tn),
        in_specs=[pl.BlockSpec((tm, D_MODEL), lambda i, j: (i, 0)),
                  pl.BlockSpec((1, D_MODEL), lambda i, j: (0, 0)),
                  pl.BlockSpec((D_MODEL, tn), lambda i, j: (0, j))],
        out_specs=pl.BlockSpec((tm, tn), lambda i, j: (i, j)),
        scratch_shapes=[pltpu.VMEM((tm, D_MODEL), BF16)],
        compiler_params=_params(("arbitrary", "arbitrary")),
        name="inproj",
    )(x, g, w)


def _softmax_sink_pv(s, sink, v):
    m = jnp.maximum(jnp.max(s, axis=1, keepdims=True), sink)
    p = jnp.exp(s - m)
    den = jnp.sum(p, axis=1, keepdims=True) + jnp.exp(sink - m)
    return jnp.dot(p.astype(BF16), v, preferred_element_type=F32) / den


def _attn_prompt_body(sink_ref, q_ref, kc_ref, kp_ref, vc_ref, vp_ref, cc_ref, sc_ref,
                      cp_ref, sp_ref, o_ref, ko_ref, vo_ref):
    i = pl.program_id(0)
    q = (_rope(q_ref[...], cc_ref[...], sc_ref[...]) * (HEAD_DIM ** -0.5)).astype(BF16)
    kc = _rope(kc_ref[...], cc_ref[...], sc_ref[...])
    kp = _rope(kp_ref[...], cp_ref[...], sp_ref[...])
    vc = vc_ref[...]
    ko_ref[...] = kc
    vo_ref[...] = vc
    kband = jnp.concatenate([kp, kc], axis=0).astype(BF16)
    vband = jnp.concatenate([vp_ref[...], vc], axis=0).astype(BF16)
    row = lax.broadcasted_iota(jnp.int32, (TOK_BLOCK, 2 * TOK_BLOCK), 0)
    col = lax.broadcasted_iota(jnp.int32, (TOK_BLOCK, 2 * TOK_BLOCK), 1)
    mask = (col <= row + WINDOW) & (col > row) & ((i > 0) | (col >= TOK_BLOCK))
    outs = []
    for hq in range(N_Q_HEADS):
        h = hq // GQA_GROUP
        qh = q[:, hq * HEAD_DIM:(hq + 1) * HEAD_DIM]
        kh = kband[:, h * HEAD_DIM:(h + 1) * HEAD_DIM]
        vh = vband[:, h * HEAD_DIM:(h + 1) * HEAD_DIM]
        s = lax.dot_general(qh, kh, (((1,), (1,)), ((), ())), preferred_element_type=F32)
        s = jnp.where(mask, s, NEG_INF)
        outs.append(_softmax_sink_pv(s, sink_ref[hq], vh))
    o_ref[...] = jnp.concatenate(outs, axis=1).astype(BF16)


def _attn_prompt(z, cos, sin, sinks, n_prompt):
    nb = n_prompt // TOK_BLOCK
    prev = lambda i: jnp.maximum(i - 1, 0)
    qb, kb, vb = COL_Q // ATTN_WIDTH, COL_K // KV_WIDTH, COL_V // KV_WIDTH
    return pl.pallas_call(
        _attn_prompt_body,
        out_shape=(jax.ShapeDtypeStruct((n_prompt, ATTN_WIDTH), BF16),
                   jax.ShapeDtypeStruct((TOK_BLOCK, KV_WIDTH), F32),
                   jax.ShapeDtypeStruct((TOK_BLOCK, KV_WIDTH), F32)),
        grid=(nb,),
        in_specs=[pl.BlockSpec(memory_space=pltpu.SMEM),
                  pl.BlockSpec((TOK_BLOCK, ATTN_WIDTH), lambda i: (i, qb)),
                  pl.BlockSpec((TOK_BLOCK, KV_WIDTH), lambda i: (i, kb)),
                  pl.BlockSpec((TOK_BLOCK, KV_WIDTH), lambda i: (prev(i), kb)),
                  pl.BlockSpec((TOK_BLOCK, KV_WIDTH), lambda i: (i, vb)),
                  pl.BlockSpec((TOK_BLOCK, KV_WIDTH), lambda i: (prev(i), vb)),
                  pl.BlockSpec((TOK_BLOCK, LANES), lambda i: (i, 0)),
                  pl.BlockSpec((TOK_BLOCK, LANES), lambda i: (i, 0)),
                  pl.BlockSpec((TOK_BLOCK, LANES), lambda i: (prev(i), 0)),
                  pl.BlockSpec((TOK_BLOCK, LANES), lambda i: (prev(i), 0))],
        out_specs=(pl.BlockSpec((TOK_BLOCK, ATTN_WIDTH), lambda i: (i, 0)),
                   pl.BlockSpec((TOK_BLOCK, KV_WIDTH), lambda i: (0, 0)),
                   pl.BlockSpec((TOK_BLOCK, KV_WIDTH), lambda i: (0, 0))),
        compiler_params=_params(("arbitrary",)),
        name="attn_prompt",
    )(sinks, z, z, z, z, z, cos, sin, cos, sin)


def _attn_sample_body(sink_ref, q_ref, k_ref, v_ref, cos_ref, sin_ref, ck_ref, cv_ref,
                      o_ref, ko_ref, vo_ref, *, seqs, dec):
    q = _rope(q_ref[...], cos_ref[...], sin_ref[...]) * (HEAD_DIM ** -0.5)
    kn = _rope(k_ref[...], cos_ref[...], sin_ref[...])
    vn = v_ref[...]
    rows = GQA_GROUP * dec
    tok_c = lax.broadcasted_iota(jnp.int32, (rows, WINDOW), 0) % dec
    col_c = lax.broadcasted_iota(jnp.int32, (rows, WINDOW), 1)
    mask_c = col_c > tok_c
    tok_n = lax.broadcasted_iota(jnp.int32, (rows, dec), 0) % dec
    col_n = lax.broadcasted_iota(jnp.int32, (rows, dec), 1)
    mask_n = col_n <= tok_n
    grp = lax.broadcasted_iota(jnp.int32, (rows, 1), 0) // dec
    out_rows = []
    for b in range(seqs):
        r0 = b * dec
        ck = ck_ref[b]
        cv = cv_ref[b]
        ko_ref[b, 0:WINDOW - dec, :] = ck[dec:, :]
        ko_ref[b, WINDOW - dec:WINDOW, :] = kn[r0:r0 + dec, :]
        vo_ref[b, 0:WINDOW - dec, :] = cv[dec:, :]
        vo_ref[b, WINDOW - dec:WINDOW, :] = vn[r0:r0 + dec, :]
        heads = []
        for h in range(N_KV_HEADS):
            lo = h * HEAD_DIM
            qs = jnp.concatenate(
                [q[r0:r0 + dec, (h * GQA_GROUP + g) * HEAD_DIM:(h * GQA_GROUP + g + 1) * HEAD_DIM]
                 for g in range(GQA_GROUP)], axis=0).astype(BF16)
            sink = jnp.zeros((rows, 1), F32)
            for g in range(GQA_GROUP):
                sink = jnp.where(grp == g, sink_ref[h * GQA_GROUP + g], sink)
            kch = ck[:, lo:lo + HEAD_DIM].astype(BF16)
            vch = cv[:, lo:lo + HEAD_DIM].astype(BF16)
            knh = kn[r0:r0 + dec, lo:lo + HEAD_DIM].astype(BF16)
            vnh = vn[r0:r0 + dec, lo:lo + HEAD_DIM].astype(BF16)
            dn = (((1,), (1,)), ((), ()))
            s_c = jnp.where(mask_c, lax.dot_general(qs, kch, dn, preferred_element_type=F32), NEG_INF)
            s_n = jnp.where(mask_n, lax.dot_general(qs, knh, dn, preferred_element_type=F32), NEG_INF)
            m = jnp.maximum(jnp.maximum(jnp.max(s_c, axis=1, keepdims=True),
                                        jnp.max(s_n, axis=1, keepdims=True)), sink)
            p_c = jnp.exp(s_c - m)
            p_n = jnp.exp(s_n - m)
            den = (jnp.sum(p_c, axis=1, keepdims=True) + jnp.sum(p_n, axis=1, keepdims=True)
                   + jnp.exp(sink - m))
            o = (jnp.dot(p_c.astype(BF16), vch, preferred_element_type=F32)
                 + jnp.dot(p_n.astype(BF16), vnh, preferred_element_type=F32)) / den
            heads.extend(o[g * dec:(g + 1) * dec, :] for g in range(GQA_GROUP))
        out_rows.append(jnp.concatenate(heads, axis=1))
    o_ref[...] = jnp.concatenate(out_rows, axis=0).astype(BF16)


def _attn_sample(z, cos, sin, sinks, cache_k, cache_v, n_prompt, dec_batch, dec):
    seqs = 8
    rb = seqs * dec
    base = n_prompt // rb
    qb, kb, vb = COL_Q // ATTN_WIDTH, COL_K // KV_WIDTH, COL_V // KV_WIDTH
    return pl.pallas_call(
        functools.partial(_attn_sample_body, seqs=seqs, dec=dec),
        out_shape=(jax.ShapeDtypeStruct((dec_batch * dec, ATTN_WIDTH), BF16),
                   jax.ShapeDtypeStruct((dec_batch, WINDOW, KV_WIDTH), F32),
                   jax.ShapeDtypeStruct((dec_batch, WINDOW, KV_WIDTH), F32)),
        grid=(dec_batch // seqs,),
        in_specs=[pl.BlockSpec(memory_space=pltpu.SMEM),
                  pl.BlockSpec((rb, ATTN_WIDTH), lambda i: (base + i, qb)),
                  pl.BlockSpec((rb, KV_WIDTH), lambda i: (base + i, kb)),
                  pl.BlockSpec((rb, KV_WIDTH), lambda i: (base + i, vb)),
                  pl.BlockSpec((rb, LANES), lambda i: (base + i, 0)),
                  pl.BlockSpec((rb, LANES), lambda i: (base + i, 0)),
                  pl.BlockSpec((seqs, WINDOW, KV_WIDTH), lambda i: (i, 0, 0)),
                  pl.BlockSpec((seqs, WINDOW, KV_WIDTH), lambda i: (i, 0, 0))],
        out_specs=(pl.BlockSpec((rb, ATTN_WIDTH), lambda i: (i, 0)),
                   pl.BlockSpec((seqs, WINDOW, KV_WIDTH), lambda i: (i, 0, 0)),
                   pl.BlockSpec((seqs, WINDOW, KV_WIDTH), lambda i: (i, 0, 0))),
        compiler_params=_params(("arbitrary",)),
        name="attn_sample",
    )(sinks, z, z, z, cos, sin, cache_k, cache_v)


def _gmlp_body(u_ref, v_ref, lg_ref, lb_ref, ws_ref, bs_ref, gm_ref, vs_ref, *, n_prompt_blocks, dec):
    i = pl.program_id(0)
    u = _gelu(u_ref[...])
    v = _gelu(v_ref[...])
    mu = jnp.mean(v, axis=-1, keepdims=True)
    var = jnp.mean(jnp.square(v - mu), axis=-1, keepdims=True)
    v = (v - mu) * lax.rsqrt(var + EPS) * lg_ref[...] + lb_ref[...]
    vs_ref[...] = v
    row = lax.broadcasted_iota(jnp.int32, (GM_CHUNK, GM_CHUNK), 0)
    col = lax.broadcasted_iota(jnp.int32, (GM_CHUNK, GM_CHUNK), 1)
    shift = jnp.where(i < n_prompt_blocks, GM_CHUNK.bit_length() - 1, dec.bit_length() - 1)
    mask = (col <= row) & ((row >> shift) == (col >> shift))
    vb = v.astype(BF16)
    outs = []
    for g in range(GM_GROUPS):
        w = jnp.where(mask, ws_ref[0, g], 0.0).astype(BF16)
        lo = g * GM_GROUP_DIM
        mix = jnp.dot(w, vb[:, lo:lo + GM_GROUP_DIM], preferred_element_type=F32) + bs_ref[0, g]
        outs.append(u[:, lo:lo + GM_GROUP_DIM] * mix)
    gm_ref[...] = jnp.concatenate(outs, axis=1).astype(BF16)


def _gmlp(z, ln_g, ln_b, w_mix, b_mix, n_tokens, n_prompt, dec):
    nb = n_tokens // TOK_BLOCK
    npb = n_prompt // TOK_BLOCK
    ub, vb = COL_U // GM_WIDTH, COL_VG // GM_WIDTH
    variant = lambda i: jnp.minimum(i // npb, 1)
    return pl.pallas_call(
        functools.partial(_gmlp_body, n_prompt_blocks=npb, dec=dec),
        out_shape=(jax.ShapeDtypeStruct((n_tokens, GM_WIDTH), BF16),
                   jax.ShapeDtypeStruct(((nb - npb + 1) * TOK_BLOCK, GM_WIDTH), F32)),
        grid=(nb,),
        in_specs=[pl.BlockSpec((TOK_BLOCK, GM_WIDTH), lambda i: (i, ub)),
                  pl.BlockSpec((TOK_BLOCK, GM_WIDTH), lambda i: (i, vb)),
                  pl.BlockSpec((1, GM_WIDTH), lambda i: (0, 0)),
                  pl.BlockSpec((1, GM_WIDTH), lambda i: (0, 0)),
                  pl.BlockSpec((1, GM_GROUPS, GM_CHUNK, GM_CHUNK), lambda i: (variant(i), 0, 0, 0)),
                  pl.BlockSpec((1, GM_GROUPS, GM_CHUNK, 1), lambda i: (variant(i), 0, 0, 0))],
        out_specs=(pl.BlockSpec((TOK_BLOCK, GM_WIDTH), lambda i: (i, 0)),
                   pl.BlockSpec((TOK_BLOCK, GM_WIDTH), lambda i: (jnp.maximum(i - (npb - 1), 0), 0))),
        compiler_params=_params(("arbitrary",)),
        name="gmlp",
    )(z, z, ln_g, ln_b, w_mix, b_mix)


def _merge_body(attn_ref, gm_ref, ga_ref, gb_ref, x_ref, woa_ref, wog_ref, wout_ref, gf_ref,
                h_ref, hn_ref):
    a = jnp.dot(attn_ref[...], woa_ref[...], preferred_element_type=F32)
    b = jnp.dot(gm_ref[...], wog_ref[...], preferred_element_type=F32)
    merged = jax.nn.sigmoid(ga_ref[...]) * a + jax.nn.sigmoid(gb_ref[...]) * b
    h = x_ref[...] + jnp.dot(merged.astype(BF16), wout_ref[...], preferred_element_type=F32)
    h_ref[...] = h
    hn_ref[...] = _rms_rows(h, gf_ref[...])


def _merge(attn, gm, z, x, w_oa, w_og, w_out, g_ffn, tm):
    n = x.shape[0]
    once = pl.Buffered(1)
    return pl.pallas_call(
        _merge_body,
        out_shape=(jax.ShapeDtypeStruct((n, D_MODEL), F32), jax.ShapeDtypeStruct((n, D_MODEL), F32)),
        grid=(n // tm,),
        in_specs=[pl.BlockSpec((tm, ATTN_WIDTH), lambda i: (i, 0)),
                  pl.BlockSpec((tm, GM_WIDTH), lambda i: (i, 0)),
                  pl.BlockSpec((tm, D_MODEL), lambda i: (i, COL_GA // D_MODEL)),
                  pl.BlockSpec((tm, D_MODEL), lambda i: (i, COL_GB // D_MODEL)),
                  pl.BlockSpec((tm, D_MODEL), lambda i: (i, 0)),
                  pl.BlockSpec((ATTN_WIDTH, D_MODEL), lambda i: (0, 0), pipeline_mode=once),
                  pl.BlockSpec((GM_WIDTH, D_MODEL), lambda i: (0, 0), pipeline_mode=once),
                  pl.BlockSpec((D_MODEL, D_MODEL), lambda i: (0, 0), pipeline_mode=once),
                  pl.BlockSpec((1, D_MODEL), lambda i: (0, 0))],
        out_specs=(pl.BlockSpec((tm, D_MODEL), lambda i: (i, 0)),
                   pl.BlockSpec((tm, D_MODEL), lambda i: (i, 0))),
        compiler_params=_params(("arbitrary",)),
        name="merge",
    )(attn, gm, z, z, x, w_oa, w_og, w_out, g_ffn)


def _topk_rows(s, k):
    n = s.shape[0]
    rows = lax.broadcasted_iota(jnp.int32, s.shape, 0).astype(F32)
    vals, ids = [], []
    for _ in range(k):
        m = jnp.max(s, axis=0, keepdims=True)
        am = jnp.min(jnp.where(s == m, rows, float(n)), axis=0, keepdims=True)
        vals.append(m)
        ids.append(am)
        s = jnp.where(rows == am, -jnp.inf, s)
    return jnp.concatenate(vals, axis=0), jnp.concatenate(ids, axis=0)


def _take_rows(table, sel, n):
    out = jnp.zeros(sel.shape, table.dtype)
    for r in range(n):
        out = jnp.where(sel == r, table[r:r + 1, :], out)
    return out


_CAND_GROUPS = ([(0, 0, 8), (0, 8, 8), (1, 0, 8)]
                + [(i, 0, PEER_TOPK // (i + 1)) for i in range(2, 8)])


def _pair_candidates(v1, v2):
    sub = lax.broadcasted_iota(jnp.int32, (8, v1.shape[1]), 0)
    groups = []
    for i, j0, valid in _CAND_GROUPS:
        g = v1[i:i + 1, :] + v2[j0:j0 + 8, :]
        groups.append(g if valid == 8 else jnp.where(sub < valid, g, -jnp.inf))
    groups.append(v1[8:16, :] + v2[0:1, :])
    return jnp.concatenate(groups, axis=0)


def _candidate_ranks(row):
    grp = jnp.floor(row * 0.125)
    sub = row - 8.0 * grp
    r1 = jnp.where(grp < 2.0, 0.0, jnp.where(grp < 9.0, grp - 1.0, row - 64.0))
    r2 = jnp.where(grp < 2.0, row, jnp.where(grp < 9.0, sub, 0.0))
    return r1, r2


def _route_body(hn_ref, wq_ref, k1_ref, k2_ref, idx_ref, gate_ref):
    q = jnp.dot(hn_ref[...].astype(BF16), wq_ref[...], preferred_element_type=F32).astype(BF16)
    k1 = k1_ref[...]
    k2 = k2_ref[...]
    dn = (((1,), (1,)), ((), ()))
    idx_out, gate_out = [], []
    for h in range(PEER_HEADS):
        lo = h * 2 * PEER_HALF
        s1 = lax.dot_general(k1, q[:, lo:lo + PEER_HALF], dn, preferred_element_type=F32)
        s2 = lax.dot_general(k2, q[:, lo + PEER_HALF:lo + 2 * PEER_HALF], dn, preferred_element_type=F32)
        v1, i1 = _topk_rows(s1, PEER_TOPK)
        v2, i2 = _topk_rows(s2, PEER_TOPK)
        sc, ci = _topk_rows(_pair_candidates(v1, v2), PEER_TOPK)
        r1, r2 = _candidate_ranks(ci)
        e1 = _take_rows(i1, r1, PEER_TOPK)
        e2 = _take_rows(i2, r2, PEER_TOPK)
        idx_out.append((e1 * float(N_KEYS) + e2).astype(jnp.int32))
        p = jnp.exp(sc - sc[0:1, :])
        gate_out.append(p / jnp.sum(p, axis=0, keepdims=True))
    idx_ref[0] = jnp.concatenate(idx_out, axis=0).T
    gate_ref[0] = jnp.concatenate(gate_out, axis=0).T


def _route(hn, w_q, keys1, keys2):
    nb = hn.shape[0] // TOK_BLOCK
    return pl.pallas_call(
        _route_body,
        out_shape=(jax.ShapeDtypeStruct((nb, TOK_BLOCK, PICKS), jnp.int32),
                   jax.ShapeDtypeStruct((nb, TOK_BLOCK, PICKS), F32)),
        grid=(nb,),
        in_specs=[pl.BlockSpec((TOK_BLOCK, D_MODEL), lambda i: (i, 0)),
                  pl.BlockSpec((D_MODEL, D_MODEL), lambda i: (0, 0), pipeline_mode=pl.Buffered(1)),
                  pl.BlockSpec((N_KEYS, PEER_HALF), lambda i: (0, 0)),
                  pl.BlockSpec((N_KEYS, PEER_HALF), lambda i: (0, 0))],
        out_specs=(pl.BlockSpec((1, TOK_BLOCK, PICKS), lambda i: (i, 0, 0)),
                   pl.BlockSpec((1, TOK_BLOCK, PICKS), lambda i: (i, 0, 0))),
        compiler_params=_params(("arbitrary",)),
        name="route",
    )(hn, w_q, keys1, keys2)


GATHER_SLOTS = 8
GATHER_AHEAD = 5
ROW_PITCH = D_CHUNKS + 1
SLOT_ROWS = PICKS * ROW_PITCH
IDX_TAIL = 8
U_HALF, V_HALF = 0, 1


def _peer_body(gate_ref, hn_ref, h_ref, gfin_ref, idx_hbm, tab_ref, yp_ref, ys_ref, *scratch,
               n_prompt_blocks):
    slots = scratch[:GATHER_SLOTS]
    sems, acc_ref, ids, id_sems = scratch[GATHER_SLOTS:]
    step = pl.program_id(0)

    def ids_copy(tok, par):
        return pltpu.make_async_copy(idx_hbm.at[step, tok], ids.at[par], id_sems.at[par])

    def issue(par, k):
        for j in range(PICKS):
            pltpu.make_async_copy(tab_ref.at[ids[par, j]],
                                  slots[k].at[pl.ds(j * ROW_PITCH, D_CHUNKS), :],
                                  sems.at[k]).start(priority=j % 2)

    def wait_all(k):
        done = slots[k].at[pl.ds(0, PICKS * D_CHUNKS), :]
        pltpu.make_async_copy(done, done, sems.at[k]).wait()

    def half(k, c, which):
        w = slots[k][pl.ds(c, PICKS, stride=ROW_PITCH), :]
        return pltpu.unpack_elementwise(w, index=which, packed_dtype=BF16, unpacked_dtype=F32)

    def mix_weights(tok, k):
        x = hn_ref[pl.ds(tok, 1), :]
        acc = None
        for c in range(D_CHUNKS):
            term = half(k, c, U_HALF) * x[:, c * LANES:(c + 1) * LANES]
            acc = term if acc is None else acc + term
        a = jnp.sum(acc.T, axis=0, keepdims=True)
        return gate_ref[0, pl.ds(tok, 1), :] * _gelu(a)

    def mix_rows(w, row, k):
        wcol = jnp.broadcast_to(w, (LANES, PICKS)).T
        outs = [jnp.sum(wcol * half(k, c, V_HALF), axis=0, keepdims=True) for c in range(D_CHUNKS)]
        acc_ref[pl.ds(row, 1), :] = jnp.concatenate(outs, axis=1)

    @pl.when(step == 0)
    def _():
        slots[GATHER_SLOTS - 1][...] = jnp.zeros((SLOT_ROWS, LANES), jnp.uint32)
        for t in range(GATHER_AHEAD):
            first = ids_copy(t, t % ID_ROWS)
            first.start()
            first.wait()
            issue(t % ID_ROWS, t)
        for t in range(GATHER_AHEAD, GATHER_AHEAD + ID_AHEAD):
            ids_copy(t, t % ID_ROWS).start()

    def group(q, w_prev):
        for k in range(GATHER_SLOTS):
            tok = q * GATHER_SLOTS + k
            nxt = tok + GATHER_AHEAD
            wait_all(k)
            ids_copy(nxt, (k + GATHER_AHEAD) % ID_ROWS).wait()
            ids_copy(nxt + ID_AHEAD, (k + GATHER_AHEAD + ID_AHEAD) % ID_ROWS).start()
            issue((k + GATHER_AHEAD) % ID_ROWS, (k + GATHER_AHEAD) % GATHER_SLOTS)
            mix_rows(w_prev, tok + 7, (k - 1) % GATHER_SLOTS)
            w_prev = mix_weights(tok, k)
        return w_prev

    w_last = lax.fori_loop(0, TOK_BLOCK // GATHER_SLOTS, group, jnp.zeros((1, PICKS), F32))
    mix_rows(w_last, TOK_BLOCK + 7, GATHER_SLOTS - 1)

    @pl.when(step == pl.num_programs(0) - 1)
    def _():
        for t in range(GATHER_AHEAD):
            wait_all(t)
        for t in range(GATHER_AHEAD, GATHER_AHEAD + ID_AHEAD):
            ids_copy(t, t % ID_ROWS).wait()

    y = _rms_rows(h_ref[...] + acc_ref[8:, :], gfin_ref[...])

    @pl.when(step < n_prompt_blocks)
    def _():
        yp_ref[...] = y

    @pl.when(step >= n_prompt_blocks)
    def _():
        ys_ref[...] = y


ID_ROWS = 4
ID_AHEAD = 2


def _peer(idx, gate, hn, h, g_final, table, n_prompt):
    n = hn.shape[0]
    nb = n // TOK_BLOCK
    npb = n_prompt // TOK_BLOCK
    assert TOK_BLOCK % GATHER_SLOTS == 0 and GATHER_SLOTS % ID_ROWS == 0 and ID_AHEAD < ID_ROWS - 1
    assert GATHER_AHEAD + ID_AHEAD <= IDX_TAIL and GATHER_AHEAD < GATHER_SLOTS - 1
    idx = jnp.concatenate([idx, jnp.roll(idx[:, :IDX_TAIL, :], -1, axis=0)], axis=1)
    return pl.pallas_call(
        functools.partial(_peer_body, n_prompt_blocks=npb),
        out_shape=(jax.ShapeDtypeStruct((n_prompt, D_MODEL), F32),
                   jax.ShapeDtypeStruct((n - n_prompt, D_MODEL), F32)),
        grid=(nb,),
        in_specs=[pl.BlockSpec((1, TOK_BLOCK, PICKS), lambda i: (i, 0, 0)),
                  pl.BlockSpec((TOK_BLOCK, D_MODEL), lambda i: (i, 0)),
                  pl.BlockSpec((TOK_BLOCK, D_MODEL), lambda i: (i, 0)),
                  pl.BlockSpec((1, D_MODEL), lambda i: (0, 0)),
                  pl.BlockSpec(memory_space=pl.ANY),
                  pl.BlockSpec(memory_space=pl.ANY)],
        out_specs=(pl.BlockSpec((TOK_BLOCK, D_MODEL), lambda i: (jnp.minimum(i, npb - 1), 0)),
                   pl.BlockSpec((TOK_BLOCK, D_MODEL), lambda i: (jnp.maximum(i - npb, 0), 0))),
        scratch_shapes=([pltpu.VMEM((SLOT_ROWS, LANES), jnp.uint32) for _ in range(GATHER_SLOTS)]
                        + [pltpu.SemaphoreType.DMA((GATHER_SLOTS,)),
                           pltpu.VMEM((8 + TOK_BLOCK, D_MODEL), F32),
                           pltpu.SMEM((ID_ROWS, PICKS), jnp.int32),
                           pltpu.SemaphoreType.DMA((ID_ROWS,))]),
        compiler_params=_params(("arbitrary",)),
        name="peer",
    )(gate, hn, h, g_final, idx, table)


def _pack_body(u_ref, v_ref, o_ref):
    for c in range(D_CHUNKS):
        sl = slice(c * LANES, (c + 1) * LANES)
        halves = {U_HALF: u_ref[:, sl], V_HALF: v_ref[:, sl]}
        o_ref[:, c, :] = pltpu.pack_elementwise([halves[0], halves[1]], packed_dtype=BF16)


def _pack_expert_table(peer_u, peer_v):
    n_exp = peer_u.shape[0]
    rows = 256
    return pl.pallas_call(
        _pack_body,
        out_shape=jax.ShapeDtypeStruct((n_exp, D_CHUNKS, LANES), jnp.uint32),
        grid=(n_exp // rows,),
        in_specs=[pl.BlockSpec((rows, D_MODEL), lambda i: (i, 0)),
                  pl.BlockSpec((rows, D_MODEL), lambda i: (i, 0))],
        out_specs=pl.BlockSpec((rows, D_CHUNKS, LANES), lambda i: (i, 0, 0)),
        compiler_params=_params(("arbitrary",)),
        name="pack_table",
    )(peer_u, peer_v)


def _rope_tables(n_prompt, n_sample, dec):
    half = HEAD_DIM // 2
    inv = 1.0 / (ROPE_THETA ** (jnp.arange(half, dtype=F32) * (2.0 / HEAD_DIM)))
    pos = jnp.concatenate([jnp.arange(n_prompt), PAST_LEN + jnp.arange(n_sample) % dec])
    ang = pos.astype(F32)[:, None] * inv[None, :]
    c, s = jnp.cos(ang), jnp.sin(ang)
    return jnp.concatenate([c, c, c, c], axis=1), jnp.concatenate([-s, s, -s, s], axis=1)


def _row_tile(n, cap):
    return max(t for t in range(TOK_BLOCK, cap + 1, TOK_BLOCK) if n % t == 0)


def kernel(x_prompt, x_sample, cache_k, cache_v, g_mix, w_in, attn_sinks, gm_ln_g, gm_ln_b,
           gm_w_s, gm_b_s, w_o_attn, w_o_gm, w_out, g_ffn, peer_w_q, peer_keys1, peer_keys2,
           peer_u, peer_v, g_final):
    depth = g_mix.shape[0]
    assert depth == 1 and x_prompt.shape[0] == 1
    n_prompt = x_prompt.shape[1]
    dec_batch, dec = x_sample.shape[0], x_sample.shape[1]
    n_sample = dec_batch * dec
    n = n_prompt + n_sample
    assert n_prompt % TOK_BLOCK == 0 and n_sample % TOK_BLOCK == 0 and TOK_BLOCK % dec == 0
    assert cache_k.shape[2] == WINDOW and dec_batch % 8 == 0

    x = jnp.concatenate([x_prompt[0], x_sample.reshape(n_sample, D_MODEL)], axis=0)
    wi = w_in[0]
    a, kv, gmw = ATTN_WIDTH, KV_WIDTH, GM_WIDTH
    w_perm = jnp.concatenate([wi[:, a + 2 * kv + 2 * gmw:], wi[:, :a], wi[:, a + 2 * kv:a + 2 * kv + 2 * gmw],
                              wi[:, a:a + 2 * kv]], axis=1).astype(BF16)
    cos, sin = _rope_tables(n_prompt, n_sample, dec)
    reps = TOK_BLOCK // dec
    w_mix = jnp.stack([gm_w_s[0], jnp.tile(gm_w_s[0][:, :dec, :dec], (1, reps, reps))])
    b_mix = jnp.stack([gm_b_s[0], jnp.tile(gm_b_s[0][:, :dec], (1, reps))])[..., None]
    table = _pack_expert_table(peer_u[0], peer_v[0])

    z = _inproj(x, g_mix, w_perm, _row_tile(n, 1536), 512)
    attn_p, k_p, v_p = _attn_prompt(z, cos, sin, attn_sinks[0], n_prompt)
    ck = cache_k[0].reshape(dec_batch, WINDOW, KV_WIDTH)
    cv = cache_v[0].reshape(dec_batch, WINDOW, KV_WIDTH)
    attn_s, k_s, v_s = _attn_sample(z, cos, sin, attn_sinks[0], ck, cv, n_prompt, dec_batch, dec)
    attn = jnp.concatenate([attn_p, attn_s], axis=0)
    gm, v_state = _gmlp(z, gm_ln_g, gm_ln_b, w_mix, b_mix, n, n_prompt, dec)
    h, hn = _merge(attn, gm, z, x, w_o_attn[0].astype(BF16), w_o_gm[0].astype(BF16),
                   w_out[0].astype(BF16), g_ffn, _row_tile(n, 256))
    idx, gate = _route(hn, peer_w_q[0].astype(BF16), peer_keys1[0].astype(BF16),
                       peer_keys2[0].astype(BF16))
    y_p, y_s = _peer(idx, gate, hn, h, g_final.reshape(1, D_MODEL), table, n_prompt)

    kv_shape = (1, 1, WINDOW, N_KV_HEADS, HEAD_DIM)
    skv_shape = (1, dec_batch, WINDOW, N_KV_HEADS, HEAD_DIM)
    return (y_p.reshape(1, n_prompt, D_MODEL),
            y_s.reshape(dec_batch, dec, D_MODEL),
            k_p.reshape(kv_shape), v_p.reshape(kv_shape),
            k_s.reshape(skv_shape), v_s.reshape(skv_shape),
            v_state[:TOK_BLOCK].reshape(1, 1, GM_CHUNK, GM_WIDTH),
            v_state[TOK_BLOCK:].reshape(1, dec_batch, dec, GM_WIDTH))
```

```python
import functools

import jax
import jax.numpy as jnp
from jax import lax
from jax.experimental import pallas as pl
from jax.experimental.pallas import tpu as pltpu

F32 = jnp.float32
BF16 = jnp.bfloat16

D_MODEL = 2048
HEAD_DIM = 64
N_Q_HEADS = 16
N_KV_HEADS = 4
GQA_GROUP = N_Q_HEADS // N_KV_HEADS
ATTN_WIDTH = N_Q_HEADS * HEAD_DIM
KV_WIDTH = N_KV_HEADS * HEAD_DIM
WINDOW = 128
GM_WIDTH = D_MODEL // 2
GM_GROUPS = 4
GM_GROUP_DIM = GM_WIDTH // GM_GROUPS
GM_CHUNK = 128
N_KEYS = 128
PEER_HEADS = 8
PEER_TOPK = 16
PEER_HALF = 128
PICKS = PEER_HEADS * PEER_TOPK
ROPE_THETA = 10000.0
PAST_LEN = 16384
EPS = 1e-6
NEG_INF = -1e30

LANES = 128
D_CHUNKS = D_MODEL // LANES
TOK_BLOCK = 128
VMEM_LIMIT = 56 * 1024 * 1024

COL_GA, COL_GB = 0, D_MODEL
COL_Q = 2 * D_MODEL
COL_U = COL_Q + ATTN_WIDTH
COL_VG = COL_U + GM_WIDTH
COL_K = COL_VG + GM_WIDTH
COL_V = COL_K + KV_WIDTH
IN_WIDTH = COL_V + KV_WIDTH


def _params(sem, **kw):
    return pltpu.CompilerParams(dimension_semantics=sem, vmem_limit_bytes=VMEM_LIMIT, **kw)


def _gelu(x):
    return 0.5 * x * (1.0 + lax.erf(x * 0.7071067811865476))


def _rms_rows(x, g):
    return x * lax.rsqrt(jnp.mean(x * x, axis=-1, keepdims=True) + EPS) * g


def _rope(x, cos, sin):
    width = x.shape[1]
    lane = lax.broadcasted_iota(jnp.int32, x.shape, 1)
    first_half = (lane & (HEAD_DIM // 2)) == 0
    partner = jnp.where(first_half,
                        pltpu.roll(x, width - HEAD_DIM // 2, axis=1),
                        pltpu.roll(x, HEAD_DIM // 2, axis=1))
    reps = width // LANES
    if reps > 1:
        cos = jnp.concatenate([cos] * reps, axis=1)
        sin = jnp.concatenate([sin] * reps, axis=1)
    return x * cos + partner * sin


def _inproj_body(x_ref, g_ref, w_ref, z_ref, xn_ref):
    @pl.when(pl.program_id(1) == 0)
    def _():
        xn_ref[...] = _rms_rows(x_ref[...], g_ref[...]).astype(BF16)

    z_ref[...] = jnp.dot(xn_ref[...], w_ref[...], preferred_element_type=F32)


def _inproj(x, g, w, tm, tn):
    n = x.shape[0]
    return pl.pallas_call(
        _inproj_body,
        out_shape=jax.ShapeDtypeStruct((n, IN_WIDTH), F32),
        grid=(n // tm, IN_WIDTH // tn),
        in_specs=[pl.BlockSpec((tm, D_MODEL), lambda i, j: (i, 0)),
                  pl.BlockSpec((1, D_MODEL), lambda i, j: (0, 0)),
                  pl.BlockSpec((D_MODEL, tn), lambda i, j: (0, j))],
        out_specs=pl.BlockSpec((tm, tn), lambda i, j: (i, j)),
        scratch_shapes=[pltpu.VMEM((tm, D_MODEL), BF16)],
        compiler_params=_params(("arbitrary", "arbitrary")),
        name="inproj",
    )(x, g, w)


def _softmax_sink_pv(s, sink, v):
    m = jnp.maximum(jnp.max(s, axis=1, keepdims=True), sink)
    p = jnp.exp(s - m)
    den = jnp.sum(p, axis=1, keepdims=True) + jnp.exp(sink - m)
    return jnp.dot(p.astype(BF16), v, preferred_element_type=F32) / den


def _attn_prompt_body(sink_ref, q_ref, kc_ref, kp_ref, vc_ref, vp_ref, cc_ref, sc_ref,
                      cp_ref, sp_ref, o_ref, ko_ref, vo_ref):
    i = pl.program_id(0)
    q = (_rope(q_ref[...], cc_ref[...], sc_ref[...]) * (HEAD_DIM ** -0.5)).astype(BF16)
    kc = _rope(kc_ref[...], cc_ref[...], sc_ref[...])
    kp = _rope(kp_ref[...], cp_ref[...], sp_ref[...])
    vc = vc_ref[...]
    ko_ref[...] = kc
    vo_ref[...] = vc
    kband = jnp.concatenate([kp, kc], axis=0).astype(BF16)
    vband = jnp.concatenate([vp_ref[...], vc], axis=0).astype(BF16)
    row = lax.broadcasted_iota(jnp.int32, (TOK_BLOCK, 2 * TOK_BLOCK), 0)
    col = lax.broadcasted_iota(jnp.int32, (TOK_BLOCK, 2 * TOK_BLOCK), 1)
    mask = (col <= row + WINDOW) & (col > row) & ((i > 0) | (col >= TOK_BLOCK))
    outs = []
    for hq in range(N_Q_HEADS):
        h = hq // GQA_GROUP
        qh = q[:, hq * HEAD_DIM:(hq + 1) * HEAD_DIM]
        kh = kband[:, h * HEAD_DIM:(h + 1) * HEAD_DIM]
        vh = vband[:, h * HEAD_DIM:(h + 1) * HEAD_DIM]
        s = lax.dot_general(qh, kh, (((1,), (1,)), ((), ())), preferred_element_type=F32)
        s = jnp.where(mask, s, NEG_INF)
        outs.append(_softmax_sink_pv(s, sink_ref[hq], vh))
    o_ref[...] = jnp.concatenate(outs, axis=1).astype(BF16)


def _attn_prompt(z, cos, sin, sinks, n_prompt):
    nb = n_prompt // TOK_BLOCK
    prev = lambda i: jnp.maximum(i - 1, 0)
    qb, kb, vb = COL_Q // ATTN_WIDTH, COL_K // KV_WIDTH, COL_V // KV_WIDTH
    return pl.pallas_call(
        _attn_prompt_body,
        out_shape=(jax.ShapeDtypeStruct((n_prompt, ATTN_WIDTH), BF16),
                   jax.ShapeDtypeStruct((TOK_BLOCK, KV_WIDTH), F32),
                   jax.ShapeDtypeStruct((TOK_BLOCK, KV_WIDTH), F32)),
        grid=(nb,),
        in_specs=[pl.BlockSpec(memory_space=pltpu.SMEM),
                  pl.BlockSpec((TOK_BLOCK, ATTN_WIDTH), lambda i: (i, qb)),
                  pl.BlockSpec((TOK_BLOCK, KV_WIDTH), lambda i: (i, kb)),
                  pl.BlockSpec((TOK_BLOCK, KV_WIDTH), lambda i: (prev(i), kb)),
                  pl.BlockSpec((TOK_BLOCK, KV_WIDTH), lambda i: (i, vb)),
                  pl.BlockSpec((TOK_BLOCK, KV_WIDTH), lambda i: (prev(i), vb)),
                  pl.BlockSpec((TOK_BLOCK, LANES), lambda i: (i, 0)),
                  pl.BlockSpec((TOK_BLOCK, LANES), lambda i: (i, 0)),
                  pl.BlockSpec((TOK_BLOCK, LANES), lambda i: (prev(i), 0)),
                  pl.BlockSpec((TOK_BLOCK, LANES), lambda i: (prev(i), 0))],
        out_specs=(pl.BlockSpec((TOK_BLOCK, ATTN_WIDTH), lambda i: (i, 0)),
                   pl.BlockSpec((TOK_BLOCK, KV_WIDTH), lambda i: (0, 0)),
                   pl.BlockSpec((TOK_BLOCK, KV_WIDTH), lambda i: (0, 0))),
        compiler_params=_params(("arbitrary",)),
        name="attn_prompt",
    )(sinks, z, z, z, z, z, cos, sin, cos, sin)


def _attn_sample_body(sink_ref, q_ref, k_ref, v_ref, cos_ref, sin_ref, ck_ref, cv_ref,
                      o_ref, ko_ref, vo_ref, *, seqs, dec):
    q = _rope(q_ref[...], cos_ref[...], sin_ref[...]) * (HEAD_DIM ** -0.5)
    kn = _rope(k_ref[...], cos_ref[...], sin_ref[...])
    vn = v_ref[...]
    rows = GQA_GROUP * dec
    tok_c = lax.broadcasted_iota(jnp.int32, (rows, WINDOW), 0) % dec
    col_c = lax.broadcasted_iota(jnp.int32, (rows, WINDOW), 1)
    mask_c = col_c > tok_c
    tok_n = lax.broadcasted_iota(jnp.int32, (rows, dec), 0) % dec
    col_n = lax.broadcasted_iota(jnp.int32, (rows, dec), 1)
    mask_n = col_n <= tok_n
    grp = lax.broadcasted_iota(jnp.int32, (rows, 1), 0) // dec
    out_rows = []
    for b in range(seqs):
        r0 = b * dec
        ck = ck_ref[b]
        cv = cv_ref[b]
        ko_ref[b, 0:WINDOW - dec, :] = ck[dec:, :]
        ko_ref[b, WINDOW - dec:WINDOW, :] = kn[r0:r0 + dec, :]
        vo_ref[b, 0:WINDOW - dec, :] = cv[dec:, :]
        vo_ref[b, WINDOW - dec:WINDOW, :] = vn[r0:r0 + dec, :]
        heads = []
        for h in range(N_KV_HEADS):
            lo = h * HEAD_DIM
            qs = jnp.concatenate(
                [q[r0:r0 + dec, (h * GQA_GROUP + g) * HEAD_DIM:(h * GQA_GROUP + g + 1) * HEAD_DIM]
                 for g in range(GQA_GROUP)], axis=0).astype(BF16)
            sink = jnp.zeros((rows, 1), F32)
            for g in range(GQA_GROUP):
                sink = jnp.where(grp == g, sink_ref[h * GQA_GROUP + g], sink)
            kch = ck[:, lo:lo + HEAD_DIM].astype(BF16)
            vch = cv[:, lo:lo + HEAD_DIM].astype(BF16)
            knh = kn[r0:r0 + dec, lo:lo + HEAD_DIM].astype(BF16)
            vnh = vn[r0:r0 + dec, lo:lo + HEAD_DIM].astype(BF16)
            dn = (((1,), (1,)), ((), ()))
            s_c = jnp.where(mask_c, lax.dot_general(qs, kch, dn, preferred_element_type=F32), NEG_INF)
            s_n = jnp.where(mask_n, lax.dot_general(qs, knh, dn, preferred_element_type=F32), NEG_INF)
            m = jnp.maximum(jnp.maximum(jnp.max(s_c, axis=1, keepdims=True),
                                        jnp.max(s_n, axis=1, keepdims=True)), sink)
            p_c = jnp.exp(s_c - m)
            p_n = jnp.exp(s_n - m)
            den = (jnp.sum(p_c, axis=1, keepdims=True) + jnp.sum(p_n, axis=1, keepdims=True)
                   + jnp.exp(sink - m))
            o = (jnp.dot(p_c.astype(BF16), vch, preferred_element_type=F32)
                 + jnp.dot(p_n.astype(BF16), vnh, preferred_element_type=F32)) / den
            heads.extend(o[g * dec:(g + 1) * dec, :] for g in range(GQA_GROUP))
        out_rows.append(jnp.concatenate(heads, axis=1))
    o_ref[...] = jnp.concatenate(out_rows, axis=0).astype(BF16)


def _attn_sample(z, cos, sin, sinks, cache_k, cache_v, n_prompt, dec_batch, dec):
    seqs = 8
    rb = seqs * dec
    base = n_prompt // rb
    qb, kb, vb = COL_Q // ATTN_WIDTH, COL_K // KV_WIDTH, COL_V // KV_WIDTH
    return pl.pallas_call(
        functools.partial(_attn_sample_body, seqs=seqs, dec=dec),
        out_shape=(jax.ShapeDtypeStruct((dec_batch * dec, ATTN_WIDTH), BF16),
                   jax.ShapeDtypeStruct((dec_batch, WINDOW, KV_WIDTH), F32),
                   jax.ShapeDtypeStruct((dec_batch, WINDOW, KV_WIDTH), F32)),
        grid=(dec_batch // seqs,),
        in_specs=[pl.BlockSpec(memory_space=pltpu.SMEM),
                  pl.BlockSpec((rb, ATTN_WIDTH), lambda i: (base + i, qb)),
                  pl.BlockSpec((rb, KV_WIDTH), lambda i: (base + i, kb)),
                  pl.BlockSpec((rb, KV_WIDTH), lambda i: (base + i, vb)),
                  pl.BlockSpec((rb, LANES), lambda i: (base + i, 0)),
                  pl.BlockSpec((rb, LANES), lambda i: (base + i, 0)),
                  pl.BlockSpec((seqs, WINDOW, KV_WIDTH), lambda i: (i, 0, 0)),
                  pl.BlockSpec((seqs, WINDOW, KV_WIDTH), lambda i: (i, 0, 0))],
        out_specs=(pl.BlockSpec((rb, ATTN_WIDTH), lambda i: (i, 0)),
                   pl.BlockSpec((seqs, WINDOW, KV_WIDTH), lambda i: (i, 0, 0)),
                   pl.BlockSpec((seqs, WINDOW, KV_WIDTH), lambda i: (i, 0, 0))),
        compiler_params=_params(("arbitrary",)),
        name="attn_sample",
    )(sinks, z, z, z, cos, sin, cache_k, cache_v)


def _gmlp_body(u_ref, v_ref, lg_ref, lb_ref, ws_ref, bs_ref, gm_ref, vs_ref, *, n_prompt_blocks, dec):
    i = pl.program_id(0)
    u = _gelu(u_ref[...])
    v = _gelu(v_ref[...])
    mu = jnp.mean(v, axis=-1, keepdims=True)
    var = jnp.mean(jnp.square(v - mu), axis=-1, keepdims=True)
    v = (v - mu) * lax.rsqrt(var + EPS) * lg_ref[...] + lb_ref[...]
    vs_ref[...] = v
    row = lax.broadcasted_iota(jnp.int32, (GM_CHUNK, GM_CHUNK), 0)
    col = lax.broadcasted_iota(jnp.int32, (GM_CHUNK, GM_CHUNK), 1)
    shift = jnp.where(i < n_prompt_blocks, GM_CHUNK.bit_length() - 1, dec.bit_length() - 1)
    mask = (col <= row) & ((row >> shift) == (col >> shift))
    vb = v.astype(BF16)
    outs = []
    for g in range(GM_GROUPS):
        w = jnp.where(mask, ws_ref[0, g], 0.0).astype(BF16)
        lo = g * GM_GROUP_DIM
        mix = jnp.dot(w, vb[:, lo:lo + GM_GROUP_DIM], preferred_element_type=F32) + bs_ref[0, g]
        outs.append(u[:, lo:lo + GM_GROUP_DIM] * mix)
    gm_ref[...] = jnp.concatenate(outs, axis=1).astype(BF16)


def _gmlp(z, ln_g, ln_b, w_mix, b_mix, n_tokens, n_prompt, dec):
    nb = n_tokens // TOK_BLOCK
    npb = n_prompt // TOK_BLOCK
    ub, vb = COL_U // GM_WIDTH, COL_VG // GM_WIDTH
    variant = lambda i: jnp.minimum(i // npb, 1)
    return pl.pallas_call(
        functools.partial(_gmlp_body, n_prompt_blocks=npb, dec=dec),
        out_shape=(jax.ShapeDtypeStruct((n_tokens, GM_WIDTH), BF16),
                   jax.ShapeDtypeStruct(((nb - npb + 1) * TOK_BLOCK, GM_WIDTH), F32)),
        grid=(nb,),
        in_specs=[pl.BlockSpec((TOK_BLOCK, GM_WIDTH), lambda i: (i, ub)),
                  pl.BlockSpec((TOK_BLOCK, GM_WIDTH), lambda i: (i, vb)),
                  pl.BlockSpec((1, GM_WIDTH), lambda i: (0, 0)),
                  pl.BlockSpec((1, GM_WIDTH), lambda i: (0, 0)),
                  pl.BlockSpec((1, GM_GROUPS, GM_CHUNK, GM_CHUNK), lambda i: (variant(i), 0, 0, 0)),
                  pl.BlockSpec((1, GM_GROUPS, GM_CHUNK, 1), lambda i: (variant(i), 0, 0, 0))],
        out_specs=(pl.BlockSpec((TOK_BLOCK, GM_WIDTH), lambda i: (i, 0)),
                   pl.BlockSpec((TOK_BLOCK, GM_WIDTH), lambda i: (jnp.maximum(i - (npb - 1), 0), 0))),
        compiler_params=_params(("arbitrary",)),
        name="gmlp",
    )(z, z, ln_g, ln_b, w_mix, b_mix)


def _merge_body(attn_ref, gm_ref, ga_ref, gb_ref, x_ref, woa_ref, wog_ref, wout_ref, gf_ref,
                h_ref, hn_ref):
    a = jnp.dot(attn_ref[...], woa_ref[...], preferred_element_type=F32)
    b = jnp.dot(gm_ref[...], wog_ref[...], preferred_element_type=F32)
    merged = jax.nn.sigmoid(ga_ref[...]) * a + jax.nn.sigmoid(gb_ref[...]) * b
    h = x_ref[...] + jnp.dot(merged.astype(BF16), wout_ref[...], preferred_element_type=F32)
    h_ref[...] = h
    hn_ref[...] = _rms_rows(h, gf_ref[...])


def _merge(attn, gm, z, x, w_oa, w_og, w_out, g_ffn, tm):
    n = x.shape[0]
    once = pl.Buffered(1)
    return pl.pallas_call(
        _merge_body,
        out_shape=(jax.ShapeDtypeStruct((n, D_MODEL), F32), jax.ShapeDtypeStruct((n, D_MODEL), F32)),
        grid=(n // tm,),
        in_specs=[pl.BlockSpec((tm, ATTN_WIDTH), lambda i: (i, 0)),
                  pl.BlockSpec((tm, GM_WIDTH), lambda i: (i, 0)),
                  pl.BlockSpec((tm, D_MODEL), lambda i: (i, COL_GA // D_MODEL)),
                  pl.BlockSpec((tm, D_MODEL), lambda i: (i, COL_GB // D_MODEL)),
                  pl.BlockSpec((tm, D_MODEL), lambda i: (i, 0)),
                  pl.BlockSpec((ATTN_WIDTH, D_MODEL), lambda i: (0, 0), pipeline_mode=once),
                  pl.BlockSpec((GM_WIDTH, D_MODEL), lambda i: (0, 0), pipeline_mode=once),
                  pl.BlockSpec((D_MODEL, D_MODEL), lambda i: (0, 0), pipeline_mode=once),
                  pl.BlockSpec((1, D_MODEL), lambda i: (0, 0))],
        out_specs=(pl.BlockSpec((tm, D_MODEL), lambda i: (i, 0)),
                   pl.BlockSpec((tm, D_MODEL), lambda i: (i, 0))),
        compiler_params=_params(("arbitrary",)),
        name="merge",
    )(attn, gm, z, z, x, w_oa, w_og, w_out, g_ffn)


def _topk_rows(s, k):
    n = s.shape[0]
    rows = lax.broadcasted_iota(jnp.int32, s.shape, 0).astype(F32)
    vals, ids = [], []
    for _ in range(k):
        m = jnp.max(s, axis=0, keepdims=True)
        am = jnp.min(jnp.where(s == m, rows, float(n)), axis=0, keepdims=True)
        vals.append(m)
        ids.append(am)
        s = jnp.where(rows == am, -jnp.inf, s)
    return jnp.concatenate(vals, axis=0), jnp.concatenate(ids, axis=0)


def _take_rows(table, sel, n):
    out = jnp.zeros(sel.shape, table.dtype)
    for r in range(n):
        out = jnp.where(sel == r, table[r:r + 1, :], out)
    return out


_CAND_GROUPS = ([(0, 0, 8), (0, 8, 8), (1, 0, 8)]
                + [(i, 0, PEER_TOPK // (i + 1)) for i in range(2, 8)])


def _pair_candidates(v1, v2):
    sub = lax.broadcasted_iota(jnp.int32, (8, v1.shape[1]), 0)
    groups = []
    for i, j0, valid in _CAND_GROUPS:
        g = v1[i:i + 1, :] + v2[j0:j0 + 8, :]
        groups.append(g if valid == 8 else jnp.where(sub < valid, g, -jnp.inf))
    groups.append(v1[8:16, :] + v2[0:1, :])
    return jnp.concatenate(groups, axis=0)


def _candidate_ranks(row):
    grp = jnp.floor(row * 0.125)
    sub = row - 8.0 * grp
    r1 = jnp.where(grp < 2.0, 0.0, jnp.where(grp < 9.0, grp - 1.0, row - 64.0))
    r2 = jnp.where(grp < 2.0, row, jnp.where(grp < 9.0, sub, 0.0))
    return r1, r2


def _route_body(hn_ref, wq_ref, k1_ref, k2_ref, idx_ref, gate_ref):
    q = jnp.dot(hn_ref[...].astype(BF16), wq_ref[...], preferred_element_type=F32).astype(BF16)
    k1 = k1_ref[...]
    k2 = k2_ref[...]
    dn = (((1,), (1,)), ((), ()))
    idx_out, gate_out = [], []
    for h in range(PEER_HEADS):
        lo = h * 2 * PEER_HALF
        s1 = lax.dot_general(k1, q[:, lo:lo + PEER_HALF], dn, preferred_element_type=F32)
        s2 = lax.dot_general(k2, q[:, lo + PEER_HALF:lo + 2 * PEER_HALF], dn, preferred_element_type=F32)
        v1, i1 = _topk_rows(s1, PEER_TOPK)
        v2, i2 = _topk_rows(s2, PEER_TOPK)
        sc, ci = _topk_rows(_pair_candidates(v1, v2), PEER_TOPK)
        r1, r2 = _candidate_ranks(ci)
        e1 = _take_rows(i1, r1, PEER_TOPK)
        e2 = _take_rows(i2, r2, PEER_TOPK)
        idx_out.append((e1 * float(N_KEYS) + e2).astype(jnp.int32))
        p = jnp.exp(sc - sc[0:1, :])
        gate_out.append(p / jnp.sum(p, axis=0, keepdims=True))
    idx_ref[0] = jnp.concatenate(idx_out, axis=0).T
    gate_ref[0] = jnp.concatenate(gate_out, axis=0).T


def _route(hn, w_q, keys1, keys2):
    nb = hn.shape[0] // TOK_BLOCK
    return pl.pallas_call(
        _route_body,
        out_shape=(jax.ShapeDtypeStruct((nb, TOK_BLOCK, PICKS), jnp.int32),
                   jax.ShapeDtypeStruct((nb, TOK_BLOCK, PICKS), F32)),
        grid=(nb,),
        in_specs=[pl.BlockSpec((TOK_BLOCK, D_MODEL), lambda i: (i, 0)),
                  pl.BlockSpec((D_MODEL, D_MODEL), lambda i: (0, 0), pipeline_mode=pl.Buffered(1)),
                  pl.BlockSpec((N_KEYS, PEER_HALF), lambda i: (0, 0)),
                  pl.BlockSpec((N_KEYS, PEER_HALF), lambda i: (0, 0))],
        out_specs=(pl.BlockSpec((1, TOK_BLOCK, PICKS), lambda i: (i, 0, 0)),
                   pl.BlockSpec((1, TOK_BLOCK, PICKS), lambda i: (i, 0, 0))),
        compiler_params=_params(("arbitrary",)),
        name="route",
    )(hn, w_q, keys1, keys2)


GATHER_SLOTS = 8
WAIT_GROUP = 4
GATHER_AHEAD = 6
ROW_PITCH = D_CHUNKS + 1
SLOT_ROWS = PICKS * ROW_PITCH
IDX_TAIL = 8
U_HALF, V_HALF = 0, 1


def _peer_body(idx_ref, gate_ref, hn_ref, h_ref, gfin_ref, tab_ref, yp_ref, ys_ref, *scratch,
               n_prompt_blocks):
    n_groups = GATHER_SLOTS // WAIT_GROUP
    slots = scratch[:GATHER_SLOTS]
    sems, acc_ref, wait_extent = scratch[GATHER_SLOTS:]
    step = pl.program_id(0)

    def issue(tok, k):
        for j in range(PICKS):
            pltpu.make_async_copy(tab_ref.at[idx_ref[0, tok, j]],
                                  slots[k].at[pl.ds(j * ROW_PITCH, D_CHUNKS), :],
                                  sems.at[k // WAIT_GROUP]).start(priority=j % 2)

    def wait_tokens(g, n_tok):
        done = wait_extent.at[pl.ds(0, n_tok * PICKS * D_CHUNKS), :]
        pltpu.make_async_copy(done, done, sems.at[g]).wait()

    def half(k, c, which):
        w = slots[k][pl.ds(c, PICKS, stride=ROW_PITCH), :]
        return pltpu.unpack_elementwise(w, index=which, packed_dtype=BF16, unpacked_dtype=F32)

    def mix_weights(tok, k):
        x = hn_ref[pl.ds(tok, 1), :]
        acc = None
        for c in range(D_CHUNKS):
            term = half(k, c, U_HALF) * x[:, c * LANES:(c + 1) * LANES]
            acc = term if acc is None else acc + term
        a = jnp.sum(acc.T, axis=0, keepdims=True)
        return gate_ref[0, pl.ds(tok, 1), :] * _gelu(a)

    def mix_rows(w, row, k):
        wcol = jnp.broadcast_to(w, (LANES, PICKS)).T
        outs = [jnp.sum(wcol * half(k, c, V_HALF), axis=0, keepdims=True) for c in range(D_CHUNKS)]
        acc_ref[pl.ds(row, 1), :] = jnp.concatenate(outs, axis=1)

    @pl.when(step == 0)
    def _():
        slots[GATHER_SLOTS - 1][...] = jnp.zeros((SLOT_ROWS, LANES), jnp.uint32)
        for t in range(GATHER_AHEAD):
            issue(t, t)

    def ring(q, w_prev):
        for k in range(GATHER_SLOTS):
            tok = q * GATHER_SLOTS + k
            if k % WAIT_GROUP == 0:
                wait_tokens(k // WAIT_GROUP, WAIT_GROUP)
            issue(tok + GATHER_AHEAD, (k + GATHER_AHEAD) % GATHER_SLOTS)
            mix_rows(w_prev, tok + 7, (k - 1) % GATHER_SLOTS)
            w_prev = mix_weights(tok, k)
        return w_prev

    w_last = lax.fori_loop(0, TOK_BLOCK // GATHER_SLOTS, ring, jnp.zeros((1, PICKS), F32))
    mix_rows(w_last, TOK_BLOCK + 7, GATHER_SLOTS - 1)

    @pl.when(step == pl.num_programs(0) - 1)
    def _():
        for g in range(n_groups):
            pending = min(max(GATHER_AHEAD - g * WAIT_GROUP, 0), WAIT_GROUP)
            if pending:
                wait_tokens(g, pending)

    y = _rms_rows(h_ref[...] + acc_ref[8:, :], gfin_ref[...])

    @pl.when(step < n_prompt_blocks)
    def _():
        yp_ref[...] = y

    @pl.when(step >= n_prompt_blocks)
    def _():
        ys_ref[...] = y


def _peer(idx, gate, hn, h, g_final, table, n_prompt):
    n = hn.shape[0]
    nb = n // TOK_BLOCK
    npb = n_prompt // TOK_BLOCK
    assert TOK_BLOCK % GATHER_SLOTS == 0 and GATHER_SLOTS % WAIT_GROUP == 0
    assert WAIT_GROUP <= GATHER_AHEAD < GATHER_SLOTS - 1 and GATHER_AHEAD <= IDX_TAIL
    idx = jnp.concatenate([idx, jnp.roll(idx[:, :IDX_TAIL, :], -1, axis=0)], axis=1)
    return pl.pallas_call(
        functools.partial(_peer_body, n_prompt_blocks=npb),
        out_shape=(jax.ShapeDtypeStruct((n_prompt, D_MODEL), F32),
                   jax.ShapeDtypeStruct((n - n_prompt, D_MODEL), F32)),
        grid=(nb,),
        in_specs=[pl.BlockSpec((1, TOK_BLOCK + IDX_TAIL, PICKS), lambda i: (i, 0, 0), memory_space=pltpu.SMEM),
                  pl.BlockSpec((1, TOK_BLOCK, PICKS), lambda i: (i, 0, 0)),
                  pl.BlockSpec((TOK_BLOCK, D_MODEL), lambda i: (i, 0)),
                  pl.BlockSpec((TOK_BLOCK, D_MODEL), lambda i: (i, 0)),
                  pl.BlockSpec((1, D_MODEL), lambda i: (0, 0)),
                  pl.BlockSpec(memory_space=pl.ANY)],
        out_specs=(pl.BlockSpec((TOK_BLOCK, D_MODEL), lambda i: (jnp.minimum(i, npb - 1), 0)),
                   pl.BlockSpec((TOK_BLOCK, D_MODEL), lambda i: (jnp.maximum(i - npb, 0), 0))),
        scratch_shapes=([pltpu.VMEM((SLOT_ROWS, LANES), jnp.uint32) for _ in range(GATHER_SLOTS)]
                        + [pltpu.SemaphoreType.DMA((GATHER_SLOTS // WAIT_GROUP,)),
                           pltpu.VMEM((8 + TOK_BLOCK, D_MODEL), F32),
                           pltpu.VMEM((WAIT_GROUP * PICKS * D_CHUNKS, LANES), jnp.uint32)]),
        compiler_params=_params(("arbitrary",)),
        name="peer",
    )(idx, gate, hn, h, g_final, table)


def _pack_body(u_ref, v_ref, o_ref):
    for c in range(D_CHUNKS):
        sl = slice(c * LANES, (c + 1) * LANES)
        halves = {U_HALF: u_ref[:, sl], V_HALF: v_ref[:, sl]}
        o_ref[:, c, :] = pltpu.pack_elementwise([halves[0], halves[1]], packed_dtype=BF16)


def _pack_expert_table(peer_u, peer_v):
    n_exp = peer_u.shape[0]
    rows = 256
    return pl.pallas_call(
        _pack_body,
        out_shape=jax.ShapeDtypeStruct((n_exp, D_CHUNKS, LANES), jnp.uint32),
        grid=(n_exp // rows,),
        in_specs=[pl.BlockSpec((rows, D_MODEL), lambda i: (i, 0)),
                  pl.BlockSpec((rows, D_MODEL), lambda i: (i, 0))],
        out_specs=pl.BlockSpec((rows, D_CHUNKS, LANES), lambda i: (i, 0, 0)),
        compiler_params=_params(("arbitrary",)),
        name="pack_table",
    )(peer_u, peer_v)


def _rope_tables(n_prompt, n_sample, dec):
    half = HEAD_DIM // 2
    inv = 1.0 / (ROPE_THETA ** (jnp.arange(half, dtype=F32) * (2.0 / HEAD_DIM)))
    pos = jnp.concatenate([jnp.arange(n_prompt), PAST_LEN + jnp.arange(n_sample) % dec])
    ang = pos.astype(F32)[:, None] * inv[None, :]
    c, s = jnp.cos(ang), jnp.sin(ang)
    return jnp.concatenate([c, c, c, c], axis=1), jnp.concatenate([-s, s, -s, s], axis=1)


def _row_tile(n, cap):
    return max(t for t in range(TOK_BLOCK, cap + 1, TOK_BLOCK) if n % t == 0)


def kernel(x_prompt, x_sample, cache_k, cache_v, g_mix, w_in, attn_sinks, gm_ln_g, gm_ln_b,
           gm_w_s, gm_b_s, w_o_attn, w_o_gm, w_out, g_ffn, peer_w_q, peer_keys1, peer_keys2,
           peer_u, peer_v, g_final):
    depth = g_mix.shape[0]
    assert depth == 1 and x_prompt.shape[0] == 1
    n_prompt = x_prompt.shape[1]
    dec_batch, dec = x_sample.shape[0], x_sample.shape[1]
    n_sample = dec_batch * dec
    n = n_prompt + n_sample
    assert n_prompt % TOK_BLOCK == 0 and n_sample % TOK_BLOCK == 0 and TOK_BLOCK % dec == 0
    assert cache_k.shape[2] == WINDOW and dec_batch % 8 == 0

    x = jnp.concatenate([x_prompt[0], x_sample.reshape(n_sample, D_MODEL)], axis=0)
    wi = w_in[0]
    a, kv, gmw = ATTN_WIDTH, KV_WIDTH, GM_WIDTH
    w_perm = jnp.concatenate([wi[:, a + 2 * kv + 2 * gmw:], wi[:, :a], wi[:, a + 2 * kv:a + 2 * kv + 2 * gmw],
                              wi[:, a:a + 2 * kv]], axis=1).astype(BF16)
    cos, sin = _rope_tables(n_prompt, n_sample, dec)
    reps = TOK_BLOCK // dec
    w_mix = jnp.stack([gm_w_s[0], jnp.tile(gm_w_s[0][:, :dec, :dec], (1, reps, reps))])
    b_mix = jnp.stack([gm_b_s[0], jnp.tile(gm_b_s[0][:, :dec], (1, reps))])[..., None]
    table = _pack_expert_table(peer_u[0], peer_v[0])

    z = _inproj(x, g_mix, w_perm, _row_tile(n, 1536), 512)
    attn_p, k_p, v_p = _attn_prompt(z, cos, sin, attn_sinks[0], n_prompt)
    ck = cache_k[0].reshape(dec_batch, WINDOW, KV_WIDTH)
    cv = cache_v[0].reshape(dec_batch, WINDOW, KV_WIDTH)
    attn_s, k_s, v_s = _attn_sample(z, cos, sin, attn_sinks[0], ck, cv, n_prompt, dec_batch, dec)
    attn = jnp.concatenate([attn_p, attn_s], axis=0)
    gm, v_state = _gmlp(z, gm_ln_g, gm_ln_b, w_mix, b_mix, n, n_prompt, dec)
    h, hn = _merge(attn, gm, z, x, w_o_attn[0].astype(BF16), w_o_gm[0].astype(BF16),
                   w_out[0].astype(BF16), g_ffn, _row_tile(n, 256))
    idx, gate = _route(hn, peer_w_q[0].astype(BF16), peer_keys1[0].astype(BF16),
                       peer_keys2[0].astype(BF16))
    y_p, y_s = _peer(idx, gate, hn, h, g_final.reshape(1, D_MODEL), table, n_prompt)

    kv_shape = (1, 1, WINDOW, N_KV_HEADS, HEAD_DIM)
    skv_shape = (1, dec_batch, WINDOW, N_KV_HEADS, HEAD_DIM)
    return (y_p.reshape(1, n_prompt, D_MODEL),
            y_s.reshape(dec_batch, dec, D_MODEL),
            k_p.reshape(kv_shape), v_p.reshape(kv_shape),
            k_s.reshape(skv_shape), v_s.reshape(skv_shape),
            v_state[:TOK_BLOCK].reshape(1, 1, GM_CHUNK, GM_WIDTH),
            v_state[TOK_BLOCK:].reshape(1, dec_batch, dec, GM_WIDTH))
```

```python
import functools

import jax
import jax.numpy as jnp
from jax import lax
from jax.experimental import pallas as pl
from jax.experimental.pallas import tpu as pltpu

F32 = jnp.float32
BF16 = jnp.bfloat16

D_MODEL = 2048
HEAD_DIM = 64
N_Q_HEADS = 16
N_KV_HEADS = 4
GQA_GROUP = N_Q_HEADS // N_KV_HEADS
ATTN_WIDTH = N_Q_HEADS * HEAD_DIM
KV_WIDTH = N_KV_HEADS * HEAD_DIM
WINDOW = 128
GM_WIDTH = D_MODEL // 2
GM_GROUPS = 4
GM_GROUP_DIM = GM_WIDTH // GM_GROUPS
GM_CHUNK = 128
N_KEYS = 128
PEER_HEADS = 8
PEER_TOPK = 16
PEER_HALF = 128
PICKS = PEER_HEADS * PEER_TOPK
ROPE_THETA = 10000.0
PAST_LEN = 16384
EPS = 1e-6
NEG_INF = -1e30

LANES = 128
D_CHUNKS = D_MODEL // LANES
TOK_BLOCK = 128
VMEM_LIMIT = 56 * 1024 * 1024

COL_GA, COL_GB = 0, D_MODEL
COL_Q = 2 * D_MODEL
COL_U = COL_Q + ATTN_WIDTH
COL_VG = COL_U + GM_WIDTH
COL_K = COL_VG + GM_WIDTH
COL_V = COL_K + KV_WIDTH
IN_WIDTH = COL_V + KV_WIDTH


def _params(sem, **kw):
    return pltpu.CompilerParams(dimension_semantics=sem, vmem_limit_bytes=VMEM_LIMIT, **kw)


def _gelu(x):
    return 0.5 * x * (1.0 + lax.erf(x * 0.7071067811865476))


def _rms_rows(x, g):
    return x * lax.rsqrt(jnp.mean(x * x, axis=-1, keepdims=True) + EPS) * g


def _rope(x, cos, sin):
    width = x.shape[1]
    lane = lax.broadcasted_iota(jnp.int32, x.shape, 1)
    first_half = (lane & (HEAD_DIM // 2)) == 0
    partner = jnp.where(first_half,
                        pltpu.roll(x, width - HEAD_DIM // 2, axis=1),
                        pltpu.roll(x, HEAD_DIM // 2, axis=1))
    reps = width // LANES
    if reps > 1:
        cos = jnp.concatenate([cos] * reps, axis=1)
        sin = jnp.concatenate([sin] * reps, axis=1)
    return x * cos + partner * sin


def _inproj_body(x_ref, g_ref, w_ref, z_ref, xn_ref):
    @pl.when(pl.program_id(1) == 0)
    def _():
        xn_ref[...] = _rms_rows(x_ref[...], g_ref[...]).astype(BF16)

    z_ref[...] = jnp.dot(xn_ref[...], w_ref[...], preferred_element_type=F32).astype(BF16)


def _inproj(x, g, w, tm, tn):
    n = x.shape[0]
    return pl.pallas_call(
        _inproj_body,
        out_shape=jax.ShapeDtypeStruct((n, IN_WIDTH), BF16),
        grid=(n // tm, IN_WIDTH // tn),
        in_specs=[pl.BlockSpec((tm, D_MODEL), lambda i, j: (i, 0)),
                  pl.BlockSpec((1, D_MODEL), lambda i, j: (0, 0)),
                  pl.BlockSpec((D_MODEL, tn), lambda i, j: (0, j))],
        out_specs=pl.BlockSpec((tm, tn), lambda i, j: (i, j)),
        scratch_shapes=[pltpu.VMEM((tm, D_MODEL), BF16)],
        compiler_params=_params(("arbitrary", "arbitrary")),
        name="inproj",
    )(x, g, w)


def _softmax_sink_pv(s, sink, v):
    m = jnp.maximum(jnp.max(s, axis=1, keepdims=True), sink)
    p = jnp.exp(s - m)
    den = jnp.sum(p, axis=1, keepdims=True) + jnp.exp(sink - m)
    return jnp.dot(p.astype(BF16), v, preferred_element_type=F32) / den


def _attn_prompt_body(sink_ref, q_ref, kc_ref, kp_ref, vc_ref, vp_ref, cc_ref, sc_ref,
                      cp_ref, sp_ref, o_ref, ko_ref, vo_ref):
    i = pl.program_id(0)
    q = (_rope(q_ref[...].astype(F32), cc_ref[...], sc_ref[...]) * (HEAD_DIM ** -0.5)).astype(BF16)
    kc = _rope(kc_ref[...].astype(F32), cc_ref[...], sc_ref[...])
    kp = _rope(kp_ref[...].astype(F32), cp_ref[...], sp_ref[...])
    vc = vc_ref[...].astype(F32)
    ko_ref[...] = kc
    vo_ref[...] = vc
    kband = jnp.concatenate([kp, kc], axis=0).astype(BF16)
    vband = jnp.concatenate([vp_ref[...], vc_ref[...]], axis=0)
    row = lax.broadcasted_iota(jnp.int32, (TOK_BLOCK, 2 * TOK_BLOCK), 0)
    col = lax.broadcasted_iota(jnp.int32, (TOK_BLOCK, 2 * TOK_BLOCK), 1)
    mask = (col <= row + WINDOW) & (col > row) & ((i > 0) | (col >= TOK_BLOCK))
    outs = []
    for hq in range(N_Q_HEADS):
        h = hq // GQA_GROUP
        qh = q[:, hq * HEAD_DIM:(hq + 1) * HEAD_DIM]
        kh = kband[:, h * HEAD_DIM:(h + 1) * HEAD_DIM]
        vh = vband[:, h * HEAD_DIM:(h + 1) * HEAD_DIM]
        s = lax.dot_general(qh, kh, (((1,), (1,)), ((), ())), preferred_element_type=F32)
        s = jnp.where(mask, s, NEG_INF)
        outs.append(_softmax_sink_pv(s, sink_ref[hq], vh))
    o_ref[...] = jnp.concatenate(outs, axis=1).astype(BF16)


def _attn_prompt(z, cos, sin, sinks, n_prompt):
    nb = n_prompt // TOK_BLOCK
    prev = lambda i: jnp.maximum(i - 1, 0)
    qb, kb, vb = COL_Q // ATTN_WIDTH, COL_K // KV_WIDTH, COL_V // KV_WIDTH
    return pl.pallas_call(
        _attn_prompt_body,
        out_shape=(jax.ShapeDtypeStruct((n_prompt, ATTN_WIDTH), BF16),
                   jax.ShapeDtypeStruct((TOK_BLOCK, KV_WIDTH), F32),
                   jax.ShapeDtypeStruct((TOK_BLOCK, KV_WIDTH), F32)),
        grid=(nb,),
        in_specs=[pl.BlockSpec(memory_space=pltpu.SMEM),
                  pl.BlockSpec((TOK_BLOCK, ATTN_WIDTH), lambda i: (i, qb)),
                  pl.BlockSpec((TOK_BLOCK, KV_WIDTH), lambda i: (i, kb)),
                  pl.BlockSpec((TOK_BLOCK, KV_WIDTH), lambda i: (prev(i), kb)),
                  pl.BlockSpec((TOK_BLOCK, KV_WIDTH), lambda i: (i, vb)),
                  pl.BlockSpec((TOK_BLOCK, KV_WIDTH), lambda i: (prev(i), vb)),
                  pl.BlockSpec((TOK_BLOCK, LANES), lambda i: (i, 0)),
                  pl.BlockSpec((TOK_BLOCK, LANES), lambda i: (i, 0)),
                  pl.BlockSpec((TOK_BLOCK, LANES), lambda i: (prev(i), 0)),
                  pl.BlockSpec((TOK_BLOCK, LANES), lambda i: (prev(i), 0))],
        out_specs=(pl.BlockSpec((TOK_BLOCK, ATTN_WIDTH), lambda i: (i, 0)),
                   pl.BlockSpec((TOK_BLOCK, KV_WIDTH), lambda i: (0, 0)),
                   pl.BlockSpec((TOK_BLOCK, KV_WIDTH), lambda i: (0, 0))),
        compiler_params=_params(("arbitrary",)),
        name="attn_prompt",
    )(sinks, z, z, z, z, z, cos, sin, cos, sin)


def _attn_sample_body(sink_ref, q_ref, k_ref, v_ref, cos_ref, sin_ref, ck_ref, cv_ref,
                      o_ref, ko_ref, vo_ref, *, seqs, dec):
    q = _rope(q_ref[...].astype(F32), cos_ref[...], sin_ref[...]) * (HEAD_DIM ** -0.5)
    kn = _rope(k_ref[...].astype(F32), cos_ref[...], sin_ref[...])
    vn = v_ref[...].astype(F32)
    rows = GQA_GROUP * dec
    tok_c = lax.broadcasted_iota(jnp.int32, (rows, WINDOW), 0) % dec
    col_c = lax.broadcasted_iota(jnp.int32, (rows, WINDOW), 1)
    mask_c = col_c > tok_c
    tok_n = lax.broadcasted_iota(jnp.int32, (rows, dec), 0) % dec
    col_n = lax.broadcasted_iota(jnp.int32, (rows, dec), 1)
    mask_n = col_n <= tok_n
    grp = lax.broadcasted_iota(jnp.int32, (rows, 1), 0) // dec
    out_rows = []
    for b in range(seqs):
        r0 = b * dec
        ck = ck_ref[b]
        cv = cv_ref[b]
        ko_ref[b, 0:WINDOW - dec, :] = ck[dec:, :]
        ko_ref[b, WINDOW - dec:WINDOW, :] = kn[r0:r0 + dec, :]
        vo_ref[b, 0:WINDOW - dec, :] = cv[dec:, :]
        vo_ref[b, WINDOW - dec:WINDOW, :] = vn[r0:r0 + dec, :]
        heads = []
        for h in range(N_KV_HEADS):
            lo = h * HEAD_DIM
            qs = jnp.concatenate(
                [q[r0:r0 + dec, (h * GQA_GROUP + g) * HEAD_DIM:(h * GQA_GROUP + g + 1) * HEAD_DIM]
                 for g in range(GQA_GROUP)], axis=0).astype(BF16)
            sink = jnp.zeros((rows, 1), F32)
            for g in range(GQA_GROUP):
                sink = jnp.where(grp == g, sink_ref[h * GQA_GROUP + g], sink)
            kch = ck[:, lo:lo + HEAD_DIM].astype(BF16)
            vch = cv[:, lo:lo + HEAD_DIM].astype(BF16)
            knh = kn[r0:r0 + dec, lo:lo + HEAD_DIM].astype(BF16)
            vnh = vn[r0:r0 + dec, lo:lo + HEAD_DIM].astype(BF16)
            dn = (((1,), (1,)), ((), ()))
            s_c = jnp.where(mask_c, lax.dot_general(qs, kch, dn, preferred_element_type=F32), NEG_INF)
            s_n = jnp.where(mask_n, lax.dot_general(qs, knh, dn, preferred_element_type=F32), NEG_INF)
            m = jnp.maximum(jnp.maximum(jnp.max(s_c, axis=1, keepdims=True),
                                        jnp.max(s_n, axis=1, keepdims=True)), sink)
            p_c = jnp.exp(s_c - m)
            p_n = jnp.exp(s_n - m)
            den = (jnp.sum(p_c, axis=1, keepdims=True) + jnp.sum(p_n, axis=1, keepdims=True)
                   + jnp.exp(sink - m))
            o = (jnp.dot(p_c.astype(BF16), vch, preferred_element_type=F32)
                 + jnp.dot(p_n.astype(BF16), vnh, preferred_element_type=F32)) / den
            heads.extend(o[g * dec:(g + 1) * dec, :] for g in range(GQA_GROUP))
        out_rows.append(jnp.concatenate(heads, axis=1))
    o_ref[...] = jnp.concatenate(out_rows, axis=0).astype(BF16)


def _attn_sample(z, cos, sin, sinks, cache_k, cache_v, n_prompt, dec_batch, dec):
    seqs = 8
    rb = seqs * dec
    base = n_prompt // rb
    qb, kb, vb = COL_Q // ATTN_WIDTH, COL_K // KV_WIDTH, COL_V // KV_WIDTH
    return pl.pallas_call(
        functools.partial(_attn_sample_body, seqs=seqs, dec=dec),
        out_shape=(jax.ShapeDtypeStruct((dec_batch * dec, ATTN_WIDTH), BF16),
                   jax.ShapeDtypeStruct((dec_batch, WINDOW, KV_WIDTH), F32),
                   jax.ShapeDtypeStruct((dec_batch, WINDOW, KV_WIDTH), F32)),
        grid=(dec_batch // seqs,),
        in_specs=[pl.BlockSpec(memory_space=pltpu.SMEM),
                  pl.BlockSpec((rb, ATTN_WIDTH), lambda i: (base + i, qb)),
                  pl.BlockSpec((rb, KV_WIDTH), lambda i: (base + i, kb)),
                  pl.BlockSpec((rb, KV_WIDTH), lambda i: (base + i, vb)),
                  pl.BlockSpec((rb, LANES), lambda i: (base + i, 0)),
                  pl.BlockSpec((rb, LANES), lambda i: (base + i, 0)),
                  pl.BlockSpec((seqs, WINDOW, KV_WIDTH), lambda i: (i, 0, 0)),
                  pl.BlockSpec((seqs, WINDOW, KV_WIDTH), lambda i: (i, 0, 0))],
        out_specs=(pl.BlockSpec((rb, ATTN_WIDTH), lambda i: (i, 0)),
                   pl.BlockSpec((seqs, WINDOW, KV_WIDTH), lambda i: (i, 0, 0)),
                   pl.BlockSpec((seqs, WINDOW, KV_WIDTH), lambda i: (i, 0, 0))),
        compiler_params=_params(("arbitrary",)),
        name="attn_sample",
    )(sinks, z, z, z, cos, sin, cache_k, cache_v)


def _gmlp_body(u_ref, v_ref, lg_ref, lb_ref, ws_ref, bs_ref, gm_ref, vs_ref, *, n_prompt_blocks, dec):
    i = pl.program_id(0)
    u = _gelu(u_ref[...].astype(F32))
    v = _gelu(v_ref[...].astype(F32))
    mu = jnp.mean(v, axis=-1, keepdims=True)
    var = jnp.mean(jnp.square(v - mu), axis=-1, keepdims=True)
    v = (v - mu) * lax.rsqrt(var + EPS) * lg_ref[...] + lb_ref[...]
    vs_ref[...] = v
    row = lax.broadcasted_iota(jnp.int32, (GM_CHUNK, GM_CHUNK), 0)
    col = lax.broadcasted_iota(jnp.int32, (GM_CHUNK, GM_CHUNK), 1)
    shift = jnp.where(i < n_prompt_blocks, GM_CHUNK.bit_length() - 1, dec.bit_length() - 1)
    mask = (col <= row) & ((row >> shift) == (col >> shift))
    vb = v.astype(BF16)
    outs = []
    for g in range(GM_GROUPS):
        w = jnp.where(mask, ws_ref[0, g], 0.0).astype(BF16)
        lo = g * GM_GROUP_DIM
        mix = jnp.dot(w, vb[:, lo:lo + GM_GROUP_DIM], preferred_element_type=F32) + bs_ref[0, g]
        outs.append(u[:, lo:lo + GM_GROUP_DIM] * mix)
    gm_ref[...] = jnp.concatenate(outs, axis=1).astype(BF16)


def _gmlp(z, ln_g, ln_b, w_mix, b_mix, n_tokens, n_prompt, dec):
    nb = n_tokens // TOK_BLOCK
    npb = n_prompt // TOK_BLOCK
    ub, vb = COL_U // GM_WIDTH, COL_VG // GM_WIDTH
    variant = lambda i: jnp.minimum(i // npb, 1)
    return pl.pallas_call(
        functools.partial(_gmlp_body, n_prompt_blocks=npb, dec=dec),
        out_shape=(jax.ShapeDtypeStruct((n_tokens, GM_WIDTH), BF16),
                   jax.ShapeDtypeStruct(((nb - npb + 1) * TOK_BLOCK, GM_WIDTH), F32)),
        grid=(nb,),
        in_specs=[pl.BlockSpec((TOK_BLOCK, GM_WIDTH), lambda i: (i, ub)),
                  pl.BlockSpec((TOK_BLOCK, GM_WIDTH), lambda i: (i, vb)),
                  pl.BlockSpec((1, GM_WIDTH), lambda i: (0, 0)),
                  pl.BlockSpec((1, GM_WIDTH), lambda i: (0, 0)),
                  pl.BlockSpec((1, GM_GROUPS, GM_CHUNK, GM_CHUNK), lambda i: (variant(i), 0, 0, 0)),
                  pl.BlockSpec((1, GM_GROUPS, GM_CHUNK, 1), lambda i: (variant(i), 0, 0, 0))],
        out_specs=(pl.BlockSpec((TOK_BLOCK, GM_WIDTH), lambda i: (i, 0)),
                   pl.BlockSpec((TOK_BLOCK, GM_WIDTH), lambda i: (jnp.maximum(i - (npb - 1), 0), 0))),
        compiler_params=_params(("arbitrary",)),
        name="gmlp",
    )(z, z, ln_g, ln_b, w_mix, b_mix)


def _merge_body(attn_ref, gm_ref, ga_ref, gb_ref, x_ref, woa_ref, wog_ref, wout_ref, gf_ref,
                h_ref, hn_ref):
    a = jnp.dot(attn_ref[...], woa_ref[...], preferred_element_type=F32)
    b = jnp.dot(gm_ref[...], wog_ref[...], preferred_element_type=F32)
    merged = jax.nn.sigmoid(ga_ref[...].astype(F32)) * a + jax.nn.sigmoid(gb_ref[...].astype(F32)) * b
    h = x_ref[...] + jnp.dot(merged.astype(BF16), wout_ref[...], preferred_element_type=F32)
    h_ref[...] = h
    hn_ref[...] = _rms_rows(h, gf_ref[...])


def _merge(attn, gm, z, x, w_oa, w_og, w_out, g_ffn, tm):
    n = x.shape[0]
    once = pl.Buffered(1)
    return pl.pallas_call(
        _merge_body,
        out_shape=(jax.ShapeDtypeStruct((n, D_MODEL), F32), jax.ShapeDtypeStruct((n, D_MODEL), F32)),
        grid=(n // tm,),
        in_specs=[pl.BlockSpec((tm, ATTN_WIDTH), lambda i: (i, 0)),
                  pl.BlockSpec((tm, GM_WIDTH), lambda i: (i, 0)),
                  pl.BlockSpec((tm, D_MODEL), lambda i: (i, COL_GA // D_MODEL)),
                  pl.BlockSpec((tm, D_MODEL), lambda i: (i, COL_GB // D_MODEL)),
                  pl.BlockSpec((tm, D_MODEL), lambda i: (i, 0)),
                  pl.BlockSpec((ATTN_WIDTH, D_MODEL), lambda i: (0, 0), pipeline_mode=once),
                  pl.BlockSpec((GM_WIDTH, D_MODEL), lambda i: (0, 0), pipeline_mode=once),
                  pl.BlockSpec((D_MODEL, D_MODEL), lambda i: (0, 0), pipeline_mode=once),
                  pl.BlockSpec((1, D_MODEL), lambda i: (0, 0))],
        out_specs=(pl.BlockSpec((tm, D_MODEL), lambda i: (i, 0)),
                   pl.BlockSpec((tm, D_MODEL), lambda i: (i, 0))),
        compiler_params=_params(("arbitrary",)),
        name="merge",
    )(attn, gm, z, z, x, w_oa, w_og, w_out, g_ffn)


def _topk_rows(s, k):
    n = s.shape[0]
    rows = lax.broadcasted_iota(jnp.int32, s.shape, 0).astype(F32)
    vals, ids = [], []
    for _ in range(k):
        m = jnp.max(s, axis=0, keepdims=True)
        am = jnp.min(jnp.where(s == m, rows, float(n)), axis=0, keepdims=True)
        vals.append(m)
        ids.append(am)
        s = jnp.where(rows == am, -jnp.inf, s)
    return jnp.concatenate(vals, axis=0), jnp.concatenate(ids, axis=0)


def _take_rows(table, sel, n):
    out = jnp.zeros(sel.shape, table.dtype)
    for r in range(n):
        out = jnp.where(sel == r, table[r:r + 1, :], out)
    return out


_CAND_GROUPS = ([(0, 0, 8), (0, 8, 8), (1, 0, 8)]
                + [(i, 0, PEER_TOPK // (i + 1)) for i in range(2, 8)])


def _pair_candidates(v1, v2):
    sub = lax.broadcasted_iota(jnp.int32, (8, v1.shape[1]), 0)
    groups = []
    for i, j0, valid in _CAND_GROUPS:
        g = v1[i:i + 1, :] + v2[j0:j0 + 8, :]
        groups.append(g if valid == 8 else jnp.where(sub < valid, g, -jnp.inf))
    groups.append(v1[8:16, :] + v2[0:1, :])
    return jnp.concatenate(groups, axis=0)


def _candidate_ranks(row):
    grp = jnp.floor(row * 0.125)
    sub = row - 8.0 * grp
    r1 = jnp.where(grp < 2.0, 0.0, jnp.where(grp < 9.0, grp - 1.0, row - 64.0))
    r2 = jnp.where(grp < 2.0, row, jnp.where(grp < 9.0, sub, 0.0))
    return r1, r2


def _route_body(hn_ref, wq_ref, k1_ref, k2_ref, idx_ref, gate_ref):
    q = jnp.dot(hn_ref[...].astype(BF16), wq_ref[...], preferred_element_type=F32).astype(BF16)
    k1 = k1_ref[...]
    k2 = k2_ref[...]
    dn = (((1,), (1,)), ((), ()))
    idx_out, gate_out = [], []
    for h in range(PEER_HEADS):
        lo = h * 2 * PEER_HALF
        s1 = lax.dot_general(k1, q[:, lo:lo + PEER_HALF], dn, preferred_element_type=F32)
        s2 = lax.dot_general(k2, q[:, lo + PEER_HALF:lo + 2 * PEER_HALF], dn, preferred_element_type=F32)
        v1, i1 = _topk_rows(s1, PEER_TOPK)
        v2, i2 = _topk_rows(s2, PEER_TOPK)
        sc, ci = _topk_rows(_pair_candidates(v1, v2), PEER_TOPK)
        r1, r2 = _candidate_ranks(ci)
        e1 = _take_rows(i1, r1, PEER_TOPK)
        e2 = _take_rows(i2, r2, PEER_TOPK)
        idx_out.append((e1 * float(N_KEYS) + e2).astype(jnp.int32))
        p = jnp.exp(sc - sc[0:1, :])
        gate_out.append(p / jnp.sum(p, axis=0, keepdims=True))
    idx_ref[0] = jnp.concatenate(idx_out, axis=0).T
    gate_ref[0] = jnp.concatenate(gate_out, axis=0).T


def _route(hn, w_q, keys1, keys2):
    nb = hn.shape[0] // TOK_BLOCK
    return pl.pallas_call(
        _route_body,
        out_shape=(jax.ShapeDtypeStruct((nb, TOK_BLOCK, PICKS), jnp.int32),
                   jax.ShapeDtypeStruct((nb, TOK_BLOCK, PICKS), F32)),
        grid=(nb,),
        in_specs=[pl.BlockSpec((TOK_BLOCK, D_MODEL), lambda i: (i, 0)),
                  pl.BlockSpec((D_MODEL, D_MODEL), lambda i: (0, 0), pipeline_mode=pl.Buffered(1)),
                  pl.BlockSpec((N_KEYS, PEER_HALF), lambda i: (0, 0)),
                  pl.BlockSpec((N_KEYS, PEER_HALF), lambda i: (0, 0))],
        out_specs=(pl.BlockSpec((1, TOK_BLOCK, PICKS), lambda i: (i, 0, 0)),
                   pl.BlockSpec((1, TOK_BLOCK, PICKS), lambda i: (i, 0, 0))),
        compiler_params=_params(("arbitrary",)),
        name="route",
    )(hn, w_q, keys1, keys2)


GATHER_SLOTS = 8
WAIT_GROUP = 1
GATHER_AHEAD = 4
ROW_PITCH = D_CHUNKS + 1
SLOT_ROWS = PICKS * ROW_PITCH
IDX_TAIL = 8
U_HALF, V_HALF = 0, 1


def _peer_body(idx_ref, gate_ref, hn_ref, h_ref, gfin_ref, tab_ref, yp_ref, ys_ref, *scratch,
               n_prompt_blocks):
    n_groups = GATHER_SLOTS // WAIT_GROUP
    slots = scratch[:GATHER_SLOTS]
    sems, acc_ref, wait_extent = scratch[GATHER_SLOTS:]
    step = pl.program_id(0)

    def issue(tok, k):
        for j in range(PICKS):
            pltpu.make_async_copy(tab_ref.at[idx_ref[0, tok, j]],
                                  slots[k].at[pl.ds(j * ROW_PITCH, D_CHUNKS), :],
                                  sems.at[k // WAIT_GROUP]).start(priority=j % 2)

    def wait_tokens(g, n_tok):
        done = wait_extent.at[pl.ds(0, n_tok * PICKS * D_CHUNKS), :]
        pltpu.make_async_copy(done, done, sems.at[g]).wait()

    def half(k, c, which):
        w = slots[k][pl.ds(c, PICKS, stride=ROW_PITCH), :]
        return pltpu.unpack_elementwise(w, index=which, packed_dtype=BF16, unpacked_dtype=F32)

    def mix_weights(tok, k):
        x = hn_ref[pl.ds(tok, 1), :]
        acc = None
        for c in range(D_CHUNKS):
            term = half(k, c, U_HALF) * x[:, c * LANES:(c + 1) * LANES]
            acc = term if acc is None else acc + term
        a = jnp.sum(acc.T, axis=0, keepdims=True)
        return gate_ref[0, pl.ds(tok, 1), :] * _gelu(a)

    def mix_rows(w, row, k):
        wcol = jnp.broadcast_to(w, (LANES, PICKS)).T
        outs = [jnp.sum(wcol * half(k, c, V_HALF), axis=0, keepdims=True) for c in range(D_CHUNKS)]
        acc_ref[pl.ds(row, 1), :] = jnp.concatenate(outs, axis=1)

    @pl.when(step == 0)
    def _():
        slots[GATHER_SLOTS - 1][...] = jnp.zeros((SLOT_ROWS, LANES), jnp.uint32)
        for t in range(GATHER_AHEAD):
            issue(t, t)

    def ring(q, w_prev):
        for k in range(GATHER_SLOTS):
            tok = q * GATHER_SLOTS + k
            if k % WAIT_GROUP == 0:
                wait_tokens(k // WAIT_GROUP, WAIT_GROUP)
            issue(tok + GATHER_AHEAD, (k + GATHER_AHEAD) % GATHER_SLOTS)
            mix_rows(w_prev, tok + 7, (k - 1) % GATHER_SLOTS)
            w_prev = mix_weights(tok, k)
        return w_prev

    w_last = lax.fori_loop(0, TOK_BLOCK // GATHER_SLOTS, ring, jnp.zeros((1, PICKS), F32))
    mix_rows(w_last, TOK_BLOCK + 7, GATHER_SLOTS - 1)

    @pl.when(step == pl.num_programs(0) - 1)
    def _():
        for g in range(n_groups):
            pending = min(max(GATHER_AHEAD - g * WAIT_GROUP, 0), WAIT_GROUP)
            if pending:
                wait_tokens(g, pending)

    y = _rms_rows(h_ref[...] + acc_ref[8:, :], gfin_ref[...])

    @pl.when(step < n_prompt_blocks)
    def _():
        yp_ref[...] = y

    @pl.when(step >= n_prompt_blocks)
    def _():
        ys_ref[...] = y


def _peer(idx, gate, hn, h, g_final, table, n_prompt):
    n = hn.shape[0]
    nb = n // TOK_BLOCK
    npb = n_prompt // TOK_BLOCK
    assert TOK_BLOCK % GATHER_SLOTS == 0 and GATHER_SLOTS % WAIT_GROUP == 0
    assert WAIT_GROUP <= GATHER_AHEAD < GATHER_SLOTS - 1 and GATHER_AHEAD <= IDX_TAIL
    idx = jnp.concatenate([idx, jnp.roll(idx[:, :IDX_TAIL, :], -1, axis=0)], axis=1)
    return pl.pallas_call(
        functools.partial(_peer_body, n_prompt_blocks=npb),
        out_shape=(jax.ShapeDtypeStruct((n_prompt, D_MODEL), F32),
                   jax.ShapeDtypeStruct((n - n_prompt, D_MODEL), F32)),
        grid=(nb,),
        in_specs=[pl.BlockSpec((1, TOK_BLOCK + IDX_TAIL, PICKS), lambda i: (i, 0, 0), memory_space=pltpu.SMEM),
                  pl.BlockSpec((1, TOK_BLOCK, PICKS), lambda i: (i, 0, 0)),
                  pl.BlockSpec((TOK_BLOCK, D_MODEL), lambda i: (i, 0)),
                  pl.BlockSpec((TOK_BLOCK, D_MODEL), lambda i: (i, 0)),
                  pl.BlockSpec((1, D_MODEL), lambda i: (0, 0)),
                  pl.BlockSpec(memory_space=pl.ANY)],
        out_specs=(pl.BlockSpec((TOK_BLOCK, D_MODEL), lambda i: (jnp.minimum(i, npb - 1), 0)),
                   pl.BlockSpec((TOK_BLOCK, D_MODEL), lambda i: (jnp.maximum(i - npb, 0), 0))),
        scratch_shapes=([pltpu.VMEM((SLOT_ROWS, LANES), jnp.uint32) for _ in range(GATHER_SLOTS)]
                        + [pltpu.SemaphoreType.DMA((GATHER_SLOTS // WAIT_GROUP,)),
                           pltpu.VMEM((8 + TOK_BLOCK, D_MODEL), F32),
                           pltpu.VMEM((WAIT_GROUP * PICKS * D_CHUNKS, LANES), jnp.uint32)]),
        compiler_params=_params(("arbitrary",)),
        name="peer",
    )(idx, gate, hn, h, g_final, table)


def _pack_body(u_ref, v_ref, o_ref):
    for c in range(D_CHUNKS):
        sl = slice(c * LANES, (c + 1) * LANES)
        halves = {U_HALF: u_ref[:, sl], V_HALF: v_ref[:, sl]}
        o_ref[:, c, :] = pltpu.pack_elementwise([halves[0], halves[1]], packed_dtype=BF16)


def _pack_expert_table(peer_u, peer_v):
    n_exp = peer_u.shape[0]
    rows = 256
    return pl.pallas_call(
        _pack_body,
        out_shape=jax.ShapeDtypeStruct((n_exp, D_CHUNKS, LANES), jnp.uint32),
        grid=(n_exp // rows,),
        in_specs=[pl.BlockSpec((rows, D_MODEL), lambda i: (i, 0)),
                  pl.BlockSpec((rows, D_MODEL), lambda i: (i, 0))],
        out_specs=pl.BlockSpec((rows, D_CHUNKS, LANES), lambda i: (i, 0, 0)),
        compiler_params=_params(("arbitrary",)),
        name="pack_table",
    )(peer_u, peer_v)


def _rope_tables(n_prompt, n_sample, dec):
    half = HEAD_DIM // 2
    inv = 1.0 / (ROPE_THETA ** (jnp.arange(half, dtype=F32) * (2.0 / HEAD_DIM)))
    pos = jnp.concatenate([jnp.arange(n_prompt), PAST_LEN + jnp.arange(n_sample) % dec])
    ang = pos.astype(F32)[:, None] * inv[None, :]
    c, s = jnp.cos(ang), jnp.sin(ang)
    return jnp.concatenate([c, c, c, c], axis=1), jnp.concatenate([-s, s, -s, s], axis=1)


def _row_tile(n, cap):
    return max(t for t in range(TOK_BLOCK, cap + 1, TOK_BLOCK) if n % t == 0)


def kernel(x_prompt, x_sample, cache_k, cache_v, g_mix, w_in, attn_sinks, gm_ln_g, gm_ln_b,
           gm_w_s, gm_b_s, w_o_attn, w_o_gm, w_out, g_ffn, peer_w_q, peer_keys1, peer_keys2,
           peer_u, peer_v, g_final):
    depth = g_mix.shape[0]
    assert depth == 1 and x_prompt.shape[0] == 1
    n_prompt = x_prompt.shape[1]
    dec_batch, dec = x_sample.shape[0], x_sample.shape[1]
    n_sample = dec_batch * dec
    n = n_prompt + n_sample
    assert n_prompt % TOK_BLOCK == 0 and n_sample % TOK_BLOCK == 0 and TOK_BLOCK % dec == 0
    assert cache_k.shape[2] == WINDOW and dec_batch % 8 == 0

    x = jnp.concatenate([x_prompt[0], x_sample.reshape(n_sample, D_MODEL)], axis=0)
    wi = w_in[0]
    a, kv, gmw = ATTN_WIDTH, KV_WIDTH, GM_WIDTH
    w_perm = jnp.concatenate([wi[:, a + 2 * kv + 2 * gmw:], wi[:, :a], wi[:, a + 2 * kv:a + 2 * kv + 2 * gmw],
                              wi[:, a:a + 2 * kv]], axis=1).astype(BF16)
    cos, sin = _rope_tables(n_prompt, n_sample, dec)
    reps = TOK_BLOCK // dec
    w_mix = jnp.stack([gm_w_s[0], jnp.tile(gm_w_s[0][:, :dec, :dec], (1, reps, reps))])
    b_mix = jnp.stack([gm_b_s[0], jnp.tile(gm_b_s[0][:, :dec], (1, reps))])[..., None]
    table = _pack_expert_table(peer_u[0], peer_v[0])

    z = _inproj(x, g_mix, w_perm, _row_tile(n, 1536), 512)
    attn_p, k_p, v_p = _attn_prompt(z, cos, sin, attn_sinks[0], n_prompt)
    ck = cache_k[0].reshape(dec_batch, WINDOW, KV_WIDTH)
    cv = cache_v[0].reshape(dec_batch, WINDOW, KV_WIDTH)
    attn_s, k_s, v_s = _attn_sample(z, cos, sin, attn_sinks[0], ck, cv, n_prompt, dec_batch, dec)
    attn = jnp.concatenate([attn_p, attn_s], axis=0)
    gm, v_state = _gmlp(z, gm_ln_g, gm_ln_b, w_mix, b_mix, n, n_prompt, dec)
    h, hn = _merge(attn, gm, z, x, w_o_attn[0].astype(BF16), w_o_gm[0].astype(BF16),
                   w_out[0].astype(BF16), g_ffn, _row_tile(n, 256))
    idx, gate = _route(hn, peer_w_q[0].astype(BF16), peer_keys1[0].astype(BF16),
                       peer_keys2[0].astype(BF16))
    y_p, y_s = _peer(idx, gate, hn, h, g_final.reshape(1, D_MODEL), table, n_prompt)

    kv_shape = (1, 1, WINDOW, N_KV_HEADS, HEAD_DIM)
    skv_shape = (1, dec_batch, WINDOW, N_KV_HEADS, HEAD_DIM)
    return (y_p.reshape(1, n_prompt, D_MODEL),
            y_s.reshape(dec_batch, dec, D_MODEL),
            k_p.reshape(kv_shape), v_p.reshape(kv_shape),
            k_s.reshape(skv_shape), v_s.reshape(skv_shape),
            v_state[:TOK_BLOCK].reshape(1, 1, GM_CHUNK, GM_WIDTH),
            v_state[TOK_BLOCK:].reshape(1, dec_batch, dec, GM_WIDTH))
```

```python
import functools

import jax
import jax.numpy as jnp
from jax import lax
from jax.experimental import pallas as pl
from jax.experimental.pallas import tpu as pltpu

F32 = jnp.float32
BF16 = jnp.bfloat16

D_MODEL = 2048
HEAD_DIM = 64
N_Q_HEADS = 16
N_KV_HEADS = 4
GQA_GROUP = N_Q_HEADS // N_KV_HEADS
ATTN_WIDTH = N_Q_HEADS * HEAD_DIM
KV_WIDTH = N_KV_HEADS * HEAD_DIM
WINDOW = 128
GM_WIDTH = D_MODEL // 2
GM_GROUPS = 4
GM_GROUP_DIM = GM_WIDTH // GM_GROUPS
GM_CHUNK = 128
N_KEYS = 128
PEER_HEADS = 8
PEER_TOPK = 16
PEER_HALF = 128
PICKS = PEER_HEADS * PEER_TOPK
ROPE_THETA = 10000.0
PAST_LEN = 16384
EPS = 1e-6
NEG_INF = -1e30

LANES = 128
D_CHUNKS = D_MODEL // LANES
TOK_BLOCK = 128
VMEM_LIMIT = 56 * 1024 * 1024

COL_GA, COL_GB = 0, D_MODEL
COL_Q = 2 * D_MODEL
COL_U = COL_Q + ATTN_WIDTH
COL_VG = COL_U + GM_WIDTH
COL_K = COL_VG + GM_WIDTH
COL_V = COL_K + KV_WIDTH
IN_WIDTH = COL_V + KV_WIDTH


def _params(sem, **kw):
    return pltpu.CompilerParams(dimension_semantics=sem, vmem_limit_bytes=VMEM_LIMIT, **kw)


def _gelu(x):
    return 0.5 * x * (1.0 + lax.erf(x * 0.7071067811865476))


def _rms_rows(x, g):
    return x * lax.rsqrt(jnp.mean(x * x, axis=-1, keepdims=True) + EPS) * g


def _rope(x, cos, sin):
    width = x.shape[1]
    lane = lax.broadcasted_iota(jnp.int32, x.shape, 1)
    first_half = (lane & (HEAD_DIM // 2)) == 0
    partner = jnp.where(first_half,
                        pltpu.roll(x, width - HEAD_DIM // 2, axis=1),
                        pltpu.roll(x, HEAD_DIM // 2, axis=1))
    reps = width // LANES
    if reps > 1:
        cos = jnp.concatenate([cos] * reps, axis=1)
        sin = jnp.concatenate([sin] * reps, axis=1)
    return x * cos + partner * sin


def _inproj_body(x_ref, g_ref, w_ref, z_ref, xn_ref):
    @pl.when(pl.program_id(1) == 0)
    def _():
        xn_ref[...] = _rms_rows(x_ref[...], g_ref[...]).astype(BF16)

    z_ref[...] = jnp.dot(xn_ref[...], w_ref[...], preferred_element_type=F32).astype(BF16)


def _inproj(x, g, w, tm, tn):
    n = x.shape[0]
    return pl.pallas_call(
        _inproj_body,
        out_shape=jax.ShapeDtypeStruct((n, IN_WIDTH), BF16),
        grid=(n // tm, IN_WIDTH // tn),
        in_specs=[pl.BlockSpec((tm, D_MODEL), lambda i, j: (i, 0)),
                  pl.BlockSpec((1, D_MODEL), lambda i, j: (0, 0)),
                  pl.BlockSpec((D_MODEL, tn), lambda i, j: (0, j))],
        out_specs=pl.BlockSpec((tm, tn), lambda i, j: (i, j)),
        scratch_shapes=[pltpu.VMEM((tm, D_MODEL), BF16)],
        compiler_params=_params(("arbitrary", "arbitrary")),
        name="inproj",
    )(x, g, w)


def _softmax_sink_pv(s, sink, v):
    m = jnp.maximum(jnp.max(s, axis=1, keepdims=True), sink)
    p = jnp.exp(s - m)
    den = jnp.sum(p, axis=1, keepdims=True) + jnp.exp(sink - m)
    return jnp.dot(p.astype(BF16), v, preferred_element_type=F32) / den


def _attn_prompt_body(sink_ref, q_ref, kc_ref, kp_ref, vc_ref, vp_ref, cc_ref, sc_ref,
                      cp_ref, sp_ref, o_ref, ko_ref, vo_ref):
    i = pl.program_id(0)
    q = (_rope(q_ref[...].astype(F32), cc_ref[...], sc_ref[...]) * (HEAD_DIM ** -0.5)).astype(BF16)
    kc = _rope(kc_ref[...].astype(F32), cc_ref[...], sc_ref[...])
    kp = _rope(kp_ref[...].astype(F32), cp_ref[...], sp_ref[...])
    vc = vc_ref[...].astype(F32)
    ko_ref[...] = kc
    vo_ref[...] = vc
    kband = jnp.concatenate([kp, kc], axis=0).astype(BF16)
    vband = jnp.concatenate([vp_ref[...], vc_ref[...]], axis=0)
    row = lax.broadcasted_iota(jnp.int32, (TOK_BLOCK, 2 * TOK_BLOCK), 0)
    col = lax.broadcasted_iota(jnp.int32, (TOK_BLOCK, 2 * TOK_BLOCK), 1)
    mask = (col <= row + WINDOW) & (col > row) & ((i > 0) | (col >= TOK_BLOCK))
    outs = []
    for hq in range(N_Q_HEADS):
        h = hq // GQA_GROUP
        qh = q[:, hq * HEAD_DIM:(hq + 1) * HEAD_DIM]
        kh = kband[:, h * HEAD_DIM:(h + 1) * HEAD_DIM]
        vh = vband[:, h * HEAD_DIM:(h + 1) * HEAD_DIM]
        s = lax.dot_general(qh, kh, (((1,), (1,)), ((), ())), preferred_element_type=F32)
        s = jnp.where(mask, s, NEG_INF)
        outs.append(_softmax_sink_pv(s, sink_ref[hq], vh))
    o_ref[...] = jnp.concatenate(outs, axis=1).astype(BF16)


def _attn_prompt(z, cos, sin, sinks, n_prompt):
    nb = n_prompt // TOK_BLOCK
    prev = lambda i: jnp.maximum(i - 1, 0)
    qb, kb, vb = COL_Q // ATTN_WIDTH, COL_K // KV_WIDTH, COL_V // KV_WIDTH
    return pl.pallas_call(
        _attn_prompt_body,
        out_shape=(jax.ShapeDtypeStruct((n_prompt, ATTN_WIDTH), BF16),
                   jax.ShapeDtypeStruct((TOK_BLOCK, KV_WIDTH), F32),
                   jax.ShapeDtypeStruct((TOK_BLOCK, KV_WIDTH), F32)),
        grid=(nb,),
        in_specs=[pl.BlockSpec(memory_space=pltpu.SMEM),
                  pl.BlockSpec((TOK_BLOCK, ATTN_WIDTH), lambda i: (i, qb)),
                  pl.BlockSpec((TOK_BLOCK, KV_WIDTH), lambda i: (i, kb)),
                  pl.BlockSpec((TOK_BLOCK, KV_WIDTH), lambda i: (prev(i), kb)),
                  pl.BlockSpec((TOK_BLOCK, KV_WIDTH), lambda i: (i, vb)),
                  pl.BlockSpec((TOK_BLOCK, KV_WIDTH), lambda i: (prev(i), vb)),
                  pl.BlockSpec((TOK_BLOCK, LANES), lambda i: (i, 0)),
                  pl.BlockSpec((TOK_BLOCK, LANES), lambda i: (i, 0)),
                  pl.BlockSpec((TOK_BLOCK, LANES), lambda i: (prev(i), 0)),
                  pl.BlockSpec((TOK_BLOCK, LANES), lambda i: (prev(i), 0))],
        out_specs=(pl.BlockSpec((TOK_BLOCK, ATTN_WIDTH), lambda i: (i, 0)),
                   pl.BlockSpec((TOK_BLOCK, KV_WIDTH), lambda i: (0, 0)),
                   pl.BlockSpec((TOK_BLOCK, KV_WIDTH), lambda i: (0, 0))),
        compiler_params=_params(("arbitrary",)),
        name="attn_prompt",
    )(sinks, z, z, z, z, z, cos, sin, cos, sin)


def _attn_sample_body(sink_ref, q_ref, k_ref, v_ref, cos_ref, sin_ref, ck_ref, cv_ref,
                      o_ref, ko_ref, vo_ref, *, seqs, dec):
    q = _rope(q_ref[...].astype(F32), cos_ref[...], sin_ref[...]) * (HEAD_DIM ** -0.5)
    kn = _rope(k_ref[...].astype(F32), cos_ref[...], sin_ref[...])
    vn = v_ref[...].astype(F32)
    rows = GQA_GROUP * dec
    tok_c = lax.broadcasted_iota(jnp.int32, (rows, WINDOW), 0) % dec
    col_c = lax.broadcasted_iota(jnp.int32, (rows, WINDOW), 1)
    mask_c = col_c > tok_c
    tok_n = lax.broadcasted_iota(jnp.int32, (rows, dec), 0) % dec
    col_n = lax.broadcasted_iota(jnp.int32, (rows, dec), 1)
    mask_n = col_n <= tok_n
    grp = lax.broadcasted_iota(jnp.int32, (rows, 1), 0) // dec
    out_rows = []
    for b in range(seqs):
        r0 = b * dec
        ck = ck_ref[b]
        cv = cv_ref[b]
        ko_ref[b, 0:WINDOW - dec, :] = ck[dec:, :]
        ko_ref[b, WINDOW - dec:WINDOW, :] = kn[r0:r0 + dec, :]
        vo_ref[b, 0:WINDOW - dec, :] = cv[dec:, :]
        vo_ref[b, WINDOW - dec:WINDOW, :] = vn[r0:r0 + dec, :]
        heads = []
        for h in range(N_KV_HEADS):
            lo = h * HEAD_DIM
            qs = jnp.concatenate(
                [q[r0:r0 + dec, (h * GQA_GROUP + g) * HEAD_DIM:(h * GQA_GROUP + g + 1) * HEAD_DIM]
                 for g in range(GQA_GROUP)], axis=0).astype(BF16)
            sink = jnp.zeros((rows, 1), F32)
            for g in range(GQA_GROUP):
                sink = jnp.where(grp == g, sink_ref[h * GQA_GROUP + g], sink)
            kch = ck[:, lo:lo + HEAD_DIM].astype(BF16)
            vch = cv[:, lo:lo + HEAD_DIM].astype(BF16)
            knh = kn[r0:r0 + dec, lo:lo + HEAD_DIM].astype(BF16)
            vnh = vn[r0:r0 + dec, lo:lo + HEAD_DIM].astype(BF16)
            dn = (((1,), (1,)), ((), ()))
            s_c = jnp.where(mask_c, lax.dot_general(qs, kch, dn, preferred_element_type=F32), NEG_INF)
            s_n = jnp.where(mask_n, lax.dot_general(qs, knh, dn, preferred_element_type=F32), NEG_INF)
            m = jnp.maximum(jnp.maximum(jnp.max(s_c, axis=1, keepdims=True),
                                        jnp.max(s_n, axis=1, keepdims=True)), sink)
            p_c = jnp.exp(s_c - m)
            p_n = jnp.exp(s_n - m)
            den = (jnp.sum(p_c, axis=1, keepdims=True) + jnp.sum(p_n, axis=1, keepdims=True)
                   + jnp.exp(sink - m))
            o = (jnp.dot(p_c.astype(BF16), vch, preferred_element_type=F32)
                 + jnp.dot(p_n.astype(BF16), vnh, preferred_element_type=F32)) / den
            heads.extend(o[g * dec:(g + 1) * dec, :] for g in range(GQA_GROUP))
        out_rows.append(jnp.concatenate(heads, axis=1))
    o_ref[...] = jnp.concatenate(out_rows, axis=0).astype(BF16)


def _attn_sample(z, cos, sin, sinks, cache_k, cache_v, n_prompt, dec_batch, dec):
    seqs = 8
    rb = seqs * dec
    base = n_prompt // rb
    qb, kb, vb = COL_Q // ATTN_WIDTH, COL_K // KV_WIDTH, COL_V // KV_WIDTH
    return pl.pallas_call(
        functools.partial(_attn_sample_body, seqs=seqs, dec=dec),
        out_shape=(jax.ShapeDtypeStruct((dec_batch * dec, ATTN_WIDTH), BF16),
                   jax.ShapeDtypeStruct((dec_batch, WINDOW, KV_WIDTH), F32),
                   jax.ShapeDtypeStruct((dec_batch, WINDOW, KV_WIDTH), F32)),
        grid=(dec_batch // seqs,),
        in_specs=[pl.BlockSpec(memory_space=pltpu.SMEM),
                  pl.BlockSpec((rb, ATTN_WIDTH), lambda i: (base + i, qb)),
                  pl.BlockSpec((rb, KV_WIDTH), lambda i: (base + i, kb)),
                  pl.BlockSpec((rb, KV_WIDTH), lambda i: (base + i, vb)),
                  pl.BlockSpec((rb, LANES), lambda i: (base + i, 0)),
                  pl.BlockSpec((rb, LANES), lambda i: (base + i, 0)),
                  pl.BlockSpec((seqs, WINDOW, KV_WIDTH), lambda i: (i, 0, 0)),
                  pl.BlockSpec((seqs, WINDOW, KV_WIDTH), lambda i: (i, 0, 0))],
        out_specs=(pl.BlockSpec((rb, ATTN_WIDTH), lambda i: (i, 0)),
                   pl.BlockSpec((seqs, WINDOW, KV_WIDTH), lambda i: (i, 0, 0)),
                   pl.BlockSpec((seqs, WINDOW, KV_WIDTH), lambda i: (i, 0, 0))),
        compiler_params=_params(("arbitrary",)),
        name="attn_sample",
    )(sinks, z, z, z, cos, sin, cache_k, cache_v)


def _gmlp_body(u_ref, v_ref, lg_ref, lb_ref, ws_ref, bs_ref, gm_ref, vs_ref, *, n_prompt_blocks, dec):
    i = pl.program_id(0)
    u = _gelu(u_ref[...].astype(F32))
    v = _gelu(v_ref[...].astype(F32))
    mu = jnp.mean(v, axis=-1, keepdims=True)
    var = jnp.mean(jnp.square(v - mu), axis=-1, keepdims=True)
    v = (v - mu) * lax.rsqrt(var + EPS) * lg_ref[...] + lb_ref[...]
    vs_ref[...] = v
    row = lax.broadcasted_iota(jnp.int32, (GM_CHUNK, GM_CHUNK), 0)
    col = lax.broadcasted_iota(jnp.int32, (GM_CHUNK, GM_CHUNK), 1)
    shift = jnp.where(i < n_prompt_blocks, GM_CHUNK.bit_length() - 1, dec.bit_length() - 1)
    mask = (col <= row) & ((row >> shift) == (col >> shift))
    vb = v.astype(BF16)
    outs = []
    for g in range(GM_GROUPS):
        w = jnp.where(mask, ws_ref[0, g], 0.0).astype(BF16)
        lo = g * GM_GROUP_DIM
        mix = jnp.dot(w, vb[:, lo:lo + GM_GROUP_DIM], preferred_element_type=F32) + bs_ref[0, g]
        outs.append(u[:, lo:lo + GM_GROUP_DIM] * mix)
    gm_ref[...] = jnp.concatenate(outs, axis=1).astype(BF16)


def _gmlp(z, ln_g, ln_b, w_mix, b_mix, n_tokens, n_prompt, dec):
    nb = n_tokens // TOK_BLOCK
    npb = n_prompt // TOK_BLOCK
    ub, vb = COL_U // GM_WIDTH, COL_VG // GM_WIDTH
    variant = lambda i: jnp.minimum(i // npb, 1)
    return pl.pallas_call(
        functools.partial(_gmlp_body, n_prompt_blocks=npb, dec=dec),
        out_shape=(jax.ShapeDtypeStruct((n_tokens, GM_WIDTH), BF16),
                   jax.ShapeDtypeStruct(((nb - npb + 1) * TOK_BLOCK, GM_WIDTH), F32)),
        grid=(nb,),
        in_specs=[pl.BlockSpec((TOK_BLOCK, GM_WIDTH), lambda i: (i, ub)),
                  pl.BlockSpec((TOK_BLOCK, GM_WIDTH), lambda i: (i, vb)),
                  pl.BlockSpec((1, GM_WIDTH), lambda i: (0, 0)),
                  pl.BlockSpec((1, GM_WIDTH), lambda i: (0, 0)),
                  pl.BlockSpec((1, GM_GROUPS, GM_CHUNK, GM_CHUNK), lambda i: (variant(i), 0, 0, 0)),
                  pl.BlockSpec((1, GM_GROUPS, GM_CHUNK, 1), lambda i: (variant(i), 0, 0, 0))],
        out_specs=(pl.BlockSpec((TOK_BLOCK, GM_WIDTH), lambda i: (i, 0)),
                   pl.BlockSpec((TOK_BLOCK, GM_WIDTH), lambda i: (jnp.maximum(i - (npb - 1), 0), 0))),
        compiler_params=_params(("arbitrary",)),
        name="gmlp",
    )(z, z, ln_g, ln_b, w_mix, b_mix)


def _merge_body(attn_ref, gm_ref, ga_ref, gb_ref, x_ref, woa_ref, wog_ref, wout_ref, gf_ref,
                h_ref, hn_ref):
    a = jnp.dot(attn_ref[...], woa_ref[...], preferred_element_type=F32)
    b = jnp.dot(gm_ref[...], wog_ref[...], preferred_element_type=F32)
    merged = jax.nn.sigmoid(ga_ref[...].astype(F32)) * a + jax.nn.sigmoid(gb_ref[...].astype(F32)) * b
    h = x_ref[...] + jnp.dot(merged.astype(BF16), wout_ref[...], preferred_element_type=F32)
    h_ref[...] = h
    hn_ref[...] = _rms_rows(h, gf_ref[...])


def _merge(attn, gm, z, x, w_oa, w_og, w_out, g_ffn, tm):
    n = x.shape[0]
    once = pl.Buffered(1)
    return pl.pallas_call(
        _merge_body,
        out_shape=(jax.ShapeDtypeStruct((n, D_MODEL), F32), jax.ShapeDtypeStruct((n, D_MODEL), F32)),
        grid=(n // tm,),
        in_specs=[pl.BlockSpec((tm, ATTN_WIDTH), lambda i: (i, 0)),
                  pl.BlockSpec((tm, GM_WIDTH), lambda i: (i, 0)),
                  pl.BlockSpec((tm, D_MODEL), lambda i: (i, COL_GA // D_MODEL)),
                  pl.BlockSpec((tm, D_MODEL), lambda i: (i, COL_GB // D_MODEL)),
                  pl.BlockSpec((tm, D_MODEL), lambda i: (i, 0)),
                  pl.BlockSpec((ATTN_WIDTH, D_MODEL), lambda i: (0, 0), pipeline_mode=once),
                  pl.BlockSpec((GM_WIDTH, D_MODEL), lambda i: (0, 0), pipeline_mode=once),
                  pl.BlockSpec((D_MODEL, D_MODEL), lambda i: (0, 0), pipeline_mode=once),
                  pl.BlockSpec((1, D_MODEL), lambda i: (0, 0))],
        out_specs=(pl.BlockSpec((tm, D_MODEL), lambda i: (i, 0)),
                   pl.BlockSpec((tm, D_MODEL), lambda i: (i, 0))),
        compiler_params=_params(("arbitrary",)),
        name="merge",
    )(attn, gm, z, z, x, w_oa, w_og, w_out, g_ffn)


def _topk_rows(s, k):
    n = s.shape[0]
    rows = lax.broadcasted_iota(jnp.int32, s.shape, 0).astype(F32)
    vals, ids = [], []
    for _ in range(k):
        m = jnp.max(s, axis=0, keepdims=True)
        am = jnp.min(jnp.where(s == m, rows, float(n)), axis=0, keepdims=True)
        vals.append(m)
        ids.append(am)
        s = jnp.where(rows == am, -jnp.inf, s)
    return jnp.concatenate(vals, axis=0), jnp.concatenate(ids, axis=0)


def _take_rows(table, sel, n):
    out = jnp.zeros(sel.shape, table.dtype)
    for r in range(n):
        out = jnp.where(sel == r, table[r:r + 1, :], out)
    return out


_CAND_GROUPS = ([(0, 0, 8), (0, 8, 8), (1, 0, 8)]
                + [(i, 0, PEER_TOPK // (i + 1)) for i in range(2, 8)])


def _pair_candidates(v1, v2):
    sub = lax.broadcasted_iota(jnp.int32, (8, v1.shape[1]), 0)
    groups = []
    for i, j0, valid in _CAND_GROUPS:
        g = v1[i:i + 1, :] + v2[j0:j0 + 8, :]
        groups.append(g if valid == 8 else jnp.where(sub < valid, g, -jnp.inf))
    groups.append(v1[8:16, :] + v2[0:1, :])
    return jnp.concatenate(groups, axis=0)


def _candidate_ranks(row):
    grp = jnp.floor(row * 0.125)
    sub = row - 8.0 * grp
    r1 = jnp.where(grp < 2.0, 0.0, jnp.where(grp < 9.0, grp - 1.0, row - 64.0))
    r2 = jnp.where(grp < 2.0, row, jnp.where(grp < 9.0, sub, 0.0))
    return r1, r2


def _route_body(hn_ref, wq_ref, k1_ref, k2_ref, idx_ref, gate_ref):
    q = jnp.dot(hn_ref[...].astype(BF16), wq_ref[...], preferred_element_type=F32).astype(BF16)
    k1 = k1_ref[...]
    k2 = k2_ref[...]
    dn = (((1,), (1,)), ((), ()))
    idx_out, gate_out = [], []
    for h in range(PEER_HEADS):
        lo = h * 2 * PEER_HALF
        s1 = lax.dot_general(k1, q[:, lo:lo + PEER_HALF], dn, preferred_element_type=F32)
        s2 = lax.dot_general(k2, q[:, lo + PEER_HALF:lo + 2 * PEER_HALF], dn, preferred_element_type=F32)
        v1, i1 = _topk_rows(s1, PEER_TOPK)
        v2, i2 = _topk_rows(s2, PEER_TOPK)
        sc, ci = _topk_rows(_pair_candidates(v1, v2), PEER_TOPK)
        r1, r2 = _candidate_ranks(ci)
        e1 = _take_rows(i1, r1, PEER_TOPK)
        e2 = _take_rows(i2, r2, PEER_TOPK)
        idx_out.append((e1 * float(N_KEYS) + e2).astype(jnp.int32))
        p = jnp.exp(sc - sc[0:1, :])
        gate_out.append(p / jnp.sum(p, axis=0, keepdims=True))
    idx_ref[0] = jnp.concatenate(idx_out, axis=0).T
    gate_ref[0] = jnp.concatenate(gate_out, axis=0).T


def _route(hn, w_q, keys1, keys2):
    nb = hn.shape[0] // TOK_BLOCK
    return pl.pallas_call(
        _route_body,
        out_shape=(jax.ShapeDtypeStruct((nb, TOK_BLOCK, PICKS), jnp.int32),
                   jax.ShapeDtypeStruct((nb, TOK_BLOCK, PICKS), F32)),
        grid=(nb,),
        in_specs=[pl.BlockSpec((TOK_BLOCK, D_MODEL), lambda i: (i, 0)),
                  pl.BlockSpec((D_MODEL, D_MODEL), lambda i: (0, 0), pipeline_mode=pl.Buffered(1)),
                  pl.BlockSpec((N_KEYS, PEER_HALF), lambda i: (0, 0)),
                  pl.BlockSpec((N_KEYS, PEER_HALF), lambda i: (0, 0))],
        out_specs=(pl.BlockSpec((1, TOK_BLOCK, PICKS), lambda i: (i, 0, 0)),
                   pl.BlockSpec((1, TOK_BLOCK, PICKS), lambda i: (i, 0, 0))),
        compiler_params=_params(("arbitrary",)),
        name="route",
    )(hn, w_q, keys1, keys2)


GATHER_SLOTS = 16
WAIT_GROUP = 4
GATHER_AHEAD = 10
ROW_PITCH = D_CHUNKS + 1
SLOT_ROWS = PICKS * ROW_PITCH
IDX_TAIL = 16
U_HALF, V_HALF = 0, 1


def _peer_body(idx_ref, gate_ref, hn_ref, h_ref, gfin_ref, tab_ref, yp_ref, ys_ref, *scratch,
               n_prompt_blocks):
    n_groups = GATHER_SLOTS // WAIT_GROUP
    slots = scratch[:GATHER_SLOTS]
    sems, acc_ref, wait_extent = scratch[GATHER_SLOTS:]
    step = pl.program_id(0)

    def issue(tok, k):
        for j in range(PICKS):
            pltpu.make_async_copy(tab_ref.at[idx_ref[0, tok, j]],
                                  slots[k].at[pl.ds(j * ROW_PITCH, D_CHUNKS), :],
                                  sems.at[k // WAIT_GROUP]).start(priority=j % 2)

    def wait_tokens(g, n_tok):
        done = wait_extent.at[pl.ds(0, n_tok * PICKS * D_CHUNKS), :]
        pltpu.make_async_copy(done, done, sems.at[g]).wait()

    def half(k, c, which):
        w = slots[k][pl.ds(c, PICKS, stride=ROW_PITCH), :]
        return pltpu.unpack_elementwise(w, index=which, packed_dtype=BF16, unpacked_dtype=F32)

    def mix_weights(tok, k):
        x = hn_ref[pl.ds(tok, 1), :]
        acc = None
        for c in range(D_CHUNKS):
            term = half(k, c, U_HALF) * x[:, c * LANES:(c + 1) * LANES]
            acc = term if acc is None else acc + term
        a = jnp.sum(acc.T, axis=0, keepdims=True)
        return gate_ref[0, pl.ds(tok, 1), :] * _gelu(a)

    def mix_rows(w, row, k):
        wcol = jnp.broadcast_to(w, (LANES, PICKS)).T
        outs = [jnp.sum(wcol * half(k, c, V_HALF), axis=0, keepdims=True) for c in range(D_CHUNKS)]
        acc_ref[pl.ds(row, 1), :] = jnp.concatenate(outs, axis=1)

    @pl.when(step == 0)
    def _():
        slots[GATHER_SLOTS - 1][...] = jnp.zeros((SLOT_ROWS, LANES), jnp.uint32)
        for t in range(GATHER_AHEAD):
            issue(t, t)

    def ring(q, w_prev):
        for k in range(GATHER_SLOTS):
            tok = q * GATHER_SLOTS + k
            if k % WAIT_GROUP == 0:
                wait_tokens(k // WAIT_GROUP, WAIT_GROUP)
            issue(tok + GATHER_AHEAD, (k + GATHER_AHEAD) % GATHER_SLOTS)
            mix_rows(w_prev, tok + 7, (k - 1) % GATHER_SLOTS)
            w_prev = mix_weights(tok, k)
        return w_prev

    w_last = lax.fori_loop(0, TOK_BLOCK // GATHER_SLOTS, ring, jnp.zeros((1, PICKS), F32))
    mix_rows(w_last, TOK_BLOCK + 7, GATHER_SLOTS - 1)

    @pl.when(step == pl.num_programs(0) - 1)
    def _():
        for g in range(n_groups):
            pending = min(max(GATHER_AHEAD - g * WAIT_GROUP, 0), WAIT_GROUP)
            if pending:
                wait_tokens(g, pending)

    y = _rms_rows(h_ref[...] + acc_ref[8:, :], gfin_ref[...])

    @pl.when(step < n_prompt_blocks)
    def _():
        yp_ref[...] = y

    @pl.when(step >= n_prompt_blocks)
    def _():
        ys_ref[...] = y


def _peer(idx, gate, hn, h, g_final, table, n_prompt):
    n = hn.shape[0]
    nb = n // TOK_BLOCK
    npb = n_prompt // TOK_BLOCK
    assert TOK_BLOCK % GATHER_SLOTS == 0 and GATHER_SLOTS % WAIT_GROUP == 0
    assert WAIT_GROUP <= GATHER_AHEAD < GATHER_SLOTS - 1 and GATHER_AHEAD <= IDX_TAIL
    idx = jnp.concatenate([idx, jnp.roll(idx[:, :IDX_TAIL, :], -1, axis=0)], axis=1)
    return pl.pallas_call(
        functools.partial(_peer_body, n_prompt_blocks=npb),
        out_shape=(jax.ShapeDtypeStruct((n_prompt, D_MODEL), F32),
                   jax.ShapeDtypeStruct((n - n_prompt, D_MODEL), F32)),
        grid=(nb,),
        in_specs=[pl.BlockSpec((1, TOK_BLOCK + IDX_TAIL, PICKS), lambda i: (i, 0, 0), memory_space=pltpu.SMEM),
                  pl.BlockSpec((1, TOK_BLOCK, PICKS), lambda i: (i, 0, 0)),
                  pl.BlockSpec((TOK_BLOCK, D_MODEL), lambda i: (i, 0)),
                  pl.BlockSpec((TOK_BLOCK, D_MODEL), lambda i: (i, 0)),
                  pl.BlockSpec((1, D_MODEL), lambda i: (0, 0)),
                  pl.BlockSpec(memory_space=pl.ANY)],
        out_specs=(pl.BlockSpec((TOK_BLOCK, D_MODEL), lambda i: (jnp.minimum(i, npb - 1), 0)),
                   pl.BlockSpec((TOK_BLOCK, D_MODEL), lambda i: (jnp.maximum(i - npb, 0), 0))),
        scratch_shapes=([pltpu.VMEM((SLOT_ROWS, LANES), jnp.uint32) for _ in range(GATHER_SLOTS)]
                        + [pltpu.SemaphoreType.DMA((GATHER_SLOTS // WAIT_GROUP,)),
                           pltpu.VMEM((8 + TOK_BLOCK, D_MODEL), F32),
                           pltpu.VMEM((WAIT_GROUP * PICKS * D_CHUNKS, LANES), jnp.uint32)]),
        compiler_params=_params(("arbitrary",)),
        name="peer",
    )(idx, gate, hn, h, g_final, table)


def _pack_body(u_ref, v_ref, o_ref):
    for c in range(D_CHUNKS):
        sl = slice(c * LANES, (c + 1) * LANES)
        halves = {U_HALF: u_ref[:, sl], V_HALF: v_ref[:, sl]}
        o_ref[:, c, :] = pltpu.pack_elementwise([halves[0], halves[1]], packed_dtype=BF16)


def _pack_expert_table(peer_u, peer_v):
    n_exp = peer_u.shape[0]
    rows = 256
    return pl.pallas_call(
        _pack_body,
        out_shape=jax.ShapeDtypeStruct((n_exp, D_CHUNKS, LANES), jnp.uint32),
        grid=(n_exp // rows,),
        in_specs=[pl.BlockSpec((rows, D_MODEL), lambda i: (i, 0)),
                  pl.BlockSpec((rows, D_MODEL), lambda i: (i, 0))],
        out_specs=pl.BlockSpec((rows, D_CHUNKS, LANES), lambda i: (i, 0, 0)),
        compiler_params=_params(("arbitrary",)),
        name="pack_table",
    )(peer_u, peer_v)


def _rope_tables(n_prompt, n_sample, dec):
    half = HEAD_DIM // 2
    inv = 1.0 / (ROPE_THETA ** (jnp.arange(half, dtype=F32) * (2.0 / HEAD_DIM)))
    pos = jnp.concatenate([jnp.arange(n_prompt), PAST_LEN + jnp.arange(n_sample) % dec])
    ang = pos.astype(F32)[:, None] * inv[None, :]
    c, s = jnp.cos(ang), jnp.sin(ang)
    return jnp.concatenate([c, c, c, c], axis=1), jnp.concatenate([-s, s, -s, s], axis=1)


def _row_tile(n, cap):
    return max(t for t in range(TOK_BLOCK, cap + 1, TOK_BLOCK) if n % t == 0)


def kernel(x_prompt, x_sample, cache_k, cache_v, g_mix, w_in, attn_sinks, gm_ln_g, gm_ln_b,
           gm_w_s, gm_b_s, w_o_attn, w_o_gm, w_out, g_ffn, peer_w_q, peer_keys1, peer_keys2,
           peer_u, peer_v, g_final):
    depth = g_mix.shape[0]
    assert depth == 1 and x_prompt.shape[0] == 1
    n_prompt = x_prompt.shape[1]
    dec_batch, dec = x_sample.shape[0], x_sample.shape[1]
    n_sample = dec_batch * dec
    n = n_prompt + n_sample
    assert n_prompt % TOK_BLOCK == 0 and n_sample % TOK_BLOCK == 0 and TOK_BLOCK % dec == 0
    assert cache_k.shape[2] == WINDOW and dec_batch % 8 == 0

    x = jnp.concatenate([x_prompt[0], x_sample.reshape(n_sample, D_MODEL)], axis=0)
    wi = w_in[0]
    a, kv, gmw = ATTN_WIDTH, KV_WIDTH, GM_WIDTH
    w_perm = jnp.concatenate([wi[:, a + 2 * kv + 2 * gmw:], wi[:, :a], wi[:, a + 2 * kv:a + 2 * kv + 2 * gmw],
                              wi[:, a:a + 2 * kv]], axis=1).astype(BF16)
    cos, sin = _rope_tables(n_prompt, n_sample, dec)
    reps = TOK_BLOCK // dec
    w_mix = jnp.stack([gm_w_s[0], jnp.tile(gm_w_s[0][:, :dec, :dec], (1, reps, reps))])
    b_mix = jnp.stack([gm_b_s[0], jnp.tile(gm_b_s[0][:, :dec], (1, reps))])[..., None]
    table = _pack_expert_table(peer_u[0], peer_v[0])

    z = _inproj(x, g_mix, w_perm, _row_tile(n, 1536), 512)
    attn_p, k_p, v_p = _attn_prompt(z, cos, sin, attn_sinks[0], n_prompt)
    ck = cache_k[0].reshape(dec_batch, WINDOW, KV_WIDTH)
    cv = cache_v[0].reshape(dec_batch, WINDOW, KV_WIDTH)
    attn_s, k_s, v_s = _attn_sample(z, cos, sin, attn_sinks[0], ck, cv, n_prompt, dec_batch, dec)
    attn = jnp.concatenate([attn_p, attn_s], axis=0)
    gm, v_state = _gmlp(z, gm_ln_g, gm_ln_b, w_mix, b_mix, n, n_prompt, dec)
    h, hn = _merge(attn, gm, z, x, w_o_attn[0].astype(BF16), w_o_gm[0].astype(BF16),
                   w_out[0].astype(BF16), g_ffn, _row_tile(n, 256))
    idx, gate = _route(hn, peer_w_q[0].astype(BF16), peer_keys1[0].astype(BF16),
                       peer_keys2[0].astype(BF16))
    y_p, y_s = _peer(idx, gate, hn, h, g_final.reshape(1, D_MODEL), table, n_prompt)

    kv_shape = (1, 1, WINDOW, N_KV_HEADS, HEAD_DIM)
    skv_shape = (1, dec_batch, WINDOW, N_KV_HEADS, HEAD_DIM)
    return (y_p.reshape(1, n_prompt, D_MODEL),
            y_s.reshape(dec_batch, dec, D_MODEL),
            k_p.reshape(kv_shape), v_p.reshape(kv_shape),
            k_s.reshape(skv_shape), v_s.reshape(skv_shape),
            v_state[:TOK_BLOCK].reshape(1, 1, GM_CHUNK, GM_WIDTH),
            v_state[TOK_BLOCK:].reshape(1, dec_batch, dec, GM_WIDTH))
```
